```python
import math
import jax, jax.numpy as jnp
from jax import lax
import numpy as np

D_MODEL = 1024
BATCH = 1
SEQ = 16384
DEPTH = 1
DEC_BATCH = 32
DEC_SEQ = 64
PAST_LEN = 4096

CHUNK = 64
RET_HEADS = 8
RET_DK = 128
RET_DV = 128
RET_WIDTH = RET_HEADS * RET_DV
D_CONV = 1024
CONV_W = 3
D_FF = 4 * D_MODEL
ROPE_BASE = 10000.0
EPS = 1e-6
IN_SIZES = (RET_HEADS * RET_DK, RET_HEADS * RET_DK, RET_WIDTH, RET_WIDTH, D_CONV, D_CONV, D_CONV, D_MODEL, D_MODEL)
D_IN = sum(IN_SIZES)

kernel_name = "hybrid_retention_shortconv_stream_step"


def rmsnorm(x, w):
    xf = x.astype(jnp.float32)
    y = xf * lax.rsqrt(jnp.mean(xf * xf, axis=-1, keepdims=True) + EPS)
    return (y * w.astype(jnp.float32)).astype(x.dtype)


def rope(x, pos):
    half = x.shape[-1] // 2
    inv = jnp.exp(-math.log(ROPE_BASE) * jnp.arange(half, dtype=jnp.float32) / half)
    ang = pos[:, None] * inv[None, :]
    cos = jnp.cos(ang)[None, :, None, :]
    sin = jnp.sin(ang)[None, :, None, :]
    xf = x.astype(jnp.float32)
    x1, x2 = xf[..., :half], xf[..., half:]
    return jnp.concatenate([x1 * cos - x2 * sin, x2 * cos + x1 * sin], axis=-1)


def log_gammas():
    h = jnp.arange(RET_HEADS, dtype=jnp.float32)
    return jnp.log(1.0 - jnp.exp2(-5.0 - h))


def retention_chunk(S, q, k, v):
    L = q.shape[1]
    lg = log_gammas()
    i = jnp.arange(L, dtype=jnp.float32)
    intra = jnp.exp(lg[:, None, None] * jnp.abs(i[:, None] - i[None, :]))
    scores = jnp.einsum('blhd,bmhd->bhlm', q, k) * intra[None]
    o_intra = jnp.einsum('bhlm,bmhe->blhe', scores, v)
    inter_decay = jnp.exp(lg[None, :] * (i[:, None] + 1.0))
    o_inter = jnp.einsum('blhd,bhde->blhe', q, S) * inter_decay[None, :, :, None]
    kv_decay = jnp.exp(lg[None, :] * (L - 1.0 - i[:, None]))
    S_new = S * jnp.exp(lg * L)[None, :, None, None] + jnp.einsum('blhd,blhe->bhde', k * kv_decay[None, :, :, None], v)
    return S_new, o_intra + o_inter


def retention(S, q, k, v):
    b, L = q.shape[0], q.shape[1]
    if L <= CHUNK:
        return retention_chunk(S, q, k, v)
    nc = L // CHUNK
    def to_chunks(t):
        return jnp.swapaxes(t.reshape(b, nc, CHUNK, t.shape[2], t.shape[3]), 0, 1)
    S_fin, o = lax.scan(lambda s, xs: retention_chunk(s, *xs), S, (to_chunks(q), to_chunks(k), to_chunks(v)))
    o = jnp.swapaxes(o, 0, 1).reshape(b, L, RET_HEADS, RET_DV)
    return S_fin, o


def layer(x, pos, s_ret, conv_buf, norm1, w_in, ret_gn_w, conv_w, w_ret_out, w_conv_out, w_o, norm2, w_ff1, w_ff2):
    b, l, _ = x.shape
    xn = rmsnorm(x, norm1)
    proj = xn @ w_in
    splits = np.cumsum(IN_SIZES)[:-1].tolist()
    q, k, v, g, cb, cc, cx, ga, gb = jnp.split(proj, splits, axis=-1)
    qh = rope(q.reshape(b, l, RET_HEADS, RET_DK), pos)
    kh = rope(k.reshape(b, l, RET_HEADS, RET_DK), pos) * (RET_DK ** -0.5)
    vh = v.reshape(b, l, RET_HEADS, RET_DV).astype(jnp.float32)
    s_new, o = retention(s_ret.astype(jnp.float32), qh, kh, vh)
    mu = jnp.mean(o, axis=-1, keepdims=True)
    var = jnp.mean(jnp.square(o - mu), axis=-1, keepdims=True)
    o = ((o - mu) * lax.rsqrt(var + EPS)).reshape(b, l, RET_WIDTH) * ret_gn_w.astype(jnp.float32)
    ret_y = (o * jax.nn.silu(g.astype(jnp.float32))).astype(x.dtype) @ w_ret_out
    u = cc * cx
    full = jnp.concatenate([conv_buf.astype(u.dtype), u], axis=1)
    y = sum(conv_w[j] * full[:, j:j + l] for j in range(CONV_W))
    new_buf = full[:, l:]
    conv_y = (cb * y) @ w_conv_out
    mix = jax.nn.sigmoid(ga) * ret_y + jax.nn.sigmoid(gb) * conv_y
    h = x + mix @ w_o
    hn = rmsnorm(h, norm2)
    h = h + jnp.square(jax.nn.relu(hn @ w_ff1)) @ w_ff2
    return h, s_new, new_buf


def setup_inputs(seed: int = 0) -> dict:
    key = jax.random.key(seed)
    ks = jax.random.split(key, 16)
    f32 = jnp.float32
    def nrm(k, shape, scale):
        return jax.random.normal(k, shape, f32) * scale
    return {
        "x_prompt": nrm(ks[0], (BATCH, SEQ, D_MODEL), 1.0),
        "x_sample": nrm(ks[1], (DEC_BATCH, DEC_SEQ, D_MODEL), 1.0),
        "state_ret": nrm(ks[2], (DEPTH, DEC_BATCH, RET_HEADS, RET_DK, RET_DV), 1.0),
        "state_conv": nrm(ks[3], (DEPTH, DEC_BATCH, CONV_W - 1, D_CONV), 1.0),
        "norm1": 1.0 + nrm(ks[4], (DEPTH, D_MODEL), 0.02),
        "w_in": nrm(ks[5], (DEPTH, D_MODEL, D_IN), D_MODEL ** -0.5),
        "ret_gn_w": 1.0 + nrm(ks[6], (DEPTH, RET_WIDTH), 0.02),
        "conv_w": nrm(ks[7], (DEPTH, CONV_W, D_CONV), CONV_W ** -0.5),
        "w_ret_out": nrm(ks[8], (DEPTH, RET_WIDTH, D_MODEL), RET_WIDTH ** -0.5),
        "w_conv_out": nrm(ks[9], (DEPTH, D_CONV, D_MODEL), D_CONV ** -0.5),
        "w_o": nrm(ks[10], (DEPTH, D_MODEL, D_MODEL), D_MODEL ** -0.5),
        "norm2": 1.0 + nrm(ks[11], (DEPTH, D_MODEL), 0.02),
        "w_ff1": nrm(ks[12], (DEPTH, D_MODEL, D_FF), D_MODEL ** -0.5),
        "w_ff2": nrm(ks[13], (DEPTH, D_FF, D_MODEL), D_FF ** -0.5),
        "norm_f": 1.0 + nrm(ks[14], (D_MODEL,), 0.02),
    }


def reference(x_prompt, x_sample, state_ret, state_conv, norm1, w_in, ret_gn_w, conv_w, w_ret_out, w_conv_out, w_o, norm2, w_ff1, w_ff2, norm_f):
    bp, lp = x_prompt.shape[0], x_prompt.shape[1]
    ls = x_sample.shape[1]
    pos_p = jnp.arange(lp, dtype=jnp.float32)
    pos_s = PAST_LEN + jnp.arange(ls, dtype=jnp.float32)
    yp, ys = x_prompt, x_sample
    ret_p, conv_p, ret_s, conv_s = [], [], [], []
    for d in range(DEPTH):
        w = (norm1[d], w_in[d], ret_gn_w[d], conv_w[d], w_ret_out[d], w_conv_out[d], w_o[d], norm2[d], w_ff1[d], w_ff2[d])
        s0 = jnp.zeros((bp, RET_HEADS, RET_DK, RET_DV), jnp.float32)
        b0 = jnp.zeros((bp, CONV_W - 1, D_CONV), x_prompt.dtype)
        yp, sp, cp = layer(yp, pos_p, s0, b0, *w)
        ys, ss, cs = layer(ys, pos_s, state_ret[d], state_conv[d], *w)
        ret_p.append(sp.astype(state_ret.dtype))
        conv_p.append(cp.astype(state_conv.dtype))
        ret_s.append(ss.astype(state_ret.dtype))
        conv_s.append(cs.astype(state_conv.dtype))
    yp = rmsnorm(yp, norm_f)
    ys = rmsnorm(ys, norm_f)
    return (yp, ys, jnp.stack(ret_p), jnp.stack(conv_p), jnp.stack(ret_s), jnp.stack(conv_s))
```

```python
import functools
import math

import jax
import jax.numpy as jnp
from jax import lax
from jax.experimental import pallas as pl
from jax.experimental.pallas import tpu as pltpu

CHUNK = 64
CHUNK_SHIFT = 6
HEADS = 8
HEAD_DIM = 128
CONV_W = 3
ROPE_BASE = 10000.0
EPS = 1e-6
PAST_LEN = 4096
HIST_ROWS = 8

MIXER_TILE = 256
MLP_TILE = 512
VMEM_LIMIT_BYTES = 56 * 1024 * 1024

F32 = jnp.float32
BF16 = jnp.bfloat16


def _log_gamma(h):
    return math.log(1.0 - 2.0 ** (-5.0 - h))


def _rms(x, w):
    ms = jnp.mean(x * x, axis=-1, keepdims=True)
    return (x * lax.rsqrt(ms + EPS)) * w


def _dot(a, b):
    return jnp.dot(a, b, preferred_element_type=F32)


def _dot_nt(a, b):
    return lax.dot_general(a, b, (((1,), (1,)), ((), ())), preferred_element_type=F32)


def _dot_tn(a, b):
    return lax.dot_general(a, b, (((0,), (0,)), ((), ())), preferred_element_type=F32)


def _mixer_kernel(carry, tile, *refs):
    n_chunks = tile // CHUNK
    if carry:
        (x_ref, n1_ref, win_ref, gnw_ref, cw_ref, wro_ref, wco_ref, wo_ref,
         h_ref, sret_ref, sconv_ref,
         dmat, rowdec, coldec, xn_s, q_s, k_s, kd_s, v_s, sg_s, gated_s, ubuf) = refs
        sin_ret_ref = sin_conv_ref = None
    else:
        (x_ref, sin_ret_ref, sin_conv_ref, n1_ref, win_ref, gnw_ref, cw_ref, wro_ref, wco_ref, wo_ref,
         h_ref, sret_ref, sconv_ref,
         dmat, rowdec, coldec, xn_s, q_s, k_s, kd_s, v_s, sg_s, gated_s, ubuf) = refs
    d_model = x_ref.shape[-1]
    step = pl.program_id(0)
    state_len = tile if carry else CHUNK

    @pl.when(step == 0)
    def _init():
        ii = lax.broadcasted_iota(jnp.int32, (tile, tile), 0)
        jj = lax.broadcasted_iota(jnp.int32, (tile, tile), 1)
        dist = jnp.abs(ii - jj).astype(F32)
        ci, cj = ii >> CHUNK_SHIFT, jj >> CHUNK_SHIFT
        keep = (cj <= ci) if carry else (cj == ci)
        row = lax.broadcasted_iota(jnp.int32, (tile, HEAD_DIM), 0)
        loc = (row if carry else row & (CHUNK - 1)).astype(F32)
        for h in range(HEADS):
            lg = _log_gamma(h)
            dmat[h] = jnp.where(keep, jnp.exp(lg * dist), 0.0)
            rowdec[h] = jnp.exp(lg * (loc + 1.0))
            coldec[h] = jnp.exp(lg * (state_len - 1.0 - loc))
        if carry:
            sret_ref[...] = jnp.zeros_like(sret_ref)
            ubuf[0:HIST_ROWS, :] = jnp.zeros((HIST_ROWS, d_model), F32)

    x = x_ref[...]
    xn_s[...] = _rms(x, n1_ref[...]).astype(BF16)

    def proj(g):
        return _dot(xn_s[...], win_ref[:, g * d_model:(g + 1) * d_model])

    lane = lax.broadcasted_iota(jnp.int32, (tile, HEAD_DIM), 1)
    row = lax.broadcasted_iota(jnp.int32, (tile, HEAD_DIM), 0)
    half = HEAD_DIM // 2
    inv = jnp.exp((-math.log(ROPE_BASE)) * (lane & (half - 1)).astype(F32) / half)
    if carry:
        pos = (step * tile + row).astype(F32)
    else:
        pos = (PAST_LEN + (row & (CHUNK - 1))).astype(F32)
    ang = pos * inv
    cos2 = jnp.cos(ang)
    sin2 = jnp.where(lane < half, -1.0, 1.0) * jnp.sin(ang)

    def rope(t):
        return t * cos2 + pltpu.roll(t, half, 1) * sin2

    q = proj(0)
    for h in range(HEADS):
        hs = slice(h * HEAD_DIM, (h + 1) * HEAD_DIM)
        q_s[:, hs] = rope(q[:, hs]).astype(BF16)
    k = proj(1)
    for h in range(HEADS):
        hs = slice(h * HEAD_DIM, (h + 1) * HEAD_DIM)
        kr = rope(k[:, hs]) * (HEAD_DIM ** -0.5)
        k_s[:, hs] = kr.astype(BF16)
        kd_s[:, hs] = (kr * coldec[h]).astype(BF16)
    v_s[...] = proj(2).astype(BF16)
    sg_s[...] = jax.nn.silu(proj(3))

    for h in range(HEADS):
        hs = slice(h * HEAD_DIM, (h + 1) * HEAD_DIM)
        decay = math.exp(_log_gamma(h) * state_len)
        qh, kh, kdh, vh = q_s[:, hs], k_s[:, hs], kd_s[:, hs], v_s[:, hs]
        p = (_dot_nt(qh, kh) * dmat[h]).astype(BF16)
        o = _dot(p, vh)
        if carry:
            s0 = sret_ref[h]
            o = o + _dot(qh, s0.astype(BF16)) * rowdec[h]
            sret_ref[h] = s0 * decay + _dot_tn(kdh, vh)
        else:
            inter = []
            for c in range(n_chunks):
                rs = slice(c * CHUNK, (c + 1) * CHUNK)
                s0 = sin_ret_ref[c, h]
                inter.append(_dot(qh[rs], s0.astype(BF16)))
                sret_ref[c, h] = s0 * decay + _dot_tn(kdh[rs], vh[rs])
            o = o + jnp.concatenate(inter, axis=0) * rowdec[h]
        mu = jnp.mean(o, axis=-1, keepdims=True)
        d = o - mu
        var = jnp.mean(d * d, axis=-1, keepdims=True)
        on = (d * lax.rsqrt(var + EPS)) * gnw_ref[:, hs]
        gated_s[:, hs] = (on * sg_s[:, hs]).astype(BF16)
    ret_y = _dot(gated_s[...], wro_ref[...])
    mix = jax.nn.sigmoid(proj(7)) * ret_y

    u = proj(5) * proj(6)
    w0, w1, w2 = cw_ref[0:1, :], cw_ref[1:2, :], cw_ref[2:3, :]
    if carry:
        ubuf[HIST_ROWS:HIST_ROWS + tile, :] = u
        y = (w0 * ubuf[HIST_ROWS - 2:HIST_ROWS - 2 + tile, :]
             + w1 * ubuf[HIST_ROWS - 1:HIST_ROWS - 1 + tile, :]) + w2 * u
        tail = ubuf[tile:tile + HIST_ROWS, :]
        sconv_ref[...] = tail
        ubuf[0:HIST_ROWS, :] = tail
    else:
        ys = []
        for c in range(n_chunks):
            ubuf[0:HIST_ROWS, :] = sin_conv_ref[c]
            ubuf[HIST_ROWS:HIST_ROWS + CHUNK, :] = u[c * CHUNK:(c + 1) * CHUNK]
            ys.append((w0 * ubuf[HIST_ROWS - 2:HIST_ROWS - 2 + CHUNK, :]
                       + w1 * ubuf[HIST_ROWS - 1:HIST_ROWS - 1 + CHUNK, :])
                      + w2 * ubuf[HIST_ROWS:HIST_ROWS + CHUNK, :])
            sconv_ref[c] = ubuf[CHUNK:CHUNK + HIST_ROWS, :]
        y = jnp.concatenate(ys, axis=0)
    conv_y = _dot((proj(4) * y).astype(BF16), wco_ref[...])
    mix = mix + jax.nn.sigmoid(proj(8)) * conv_y

    h_ref[...] = x + _dot(mix.astype(BF16), wo_ref[...])


def _resident(shape):
    nd = len(shape)
    return pl.BlockSpec(shape, lambda i: (0,) * nd, pipeline_mode=pl.Buffered(1))


def _mixer(x, state_ret, state_conv, n1, win, gnw, cw, wro, wco, wo, *, carry):
    n, d = x.shape
    tile = MIXER_TILE
    n_chunks = tile // CHUNK
    grid = (n // tile,)
    tok = pl.BlockSpec((tile, d), lambda i: (i, 0))
    weights = [n1, win, gnw, cw, wro, wco, wo]
    w_specs = [_resident(w.shape) for w in weights]
    if carry:
        inputs = [x] + weights
        in_specs = [tok] + w_specs
        out_shape = (jax.ShapeDtypeStruct((n, d), F32),
                     jax.ShapeDtypeStruct((HEADS, HEAD_DIM, HEAD_DIM), F32),
                     jax.ShapeDtypeStruct((HIST_ROWS, d), F32))
        out_specs = (tok,
                     pl.BlockSpec((HEADS, HEAD_DIM, HEAD_DIM), lambda i: (0, 0, 0)),
                     pl.BlockSpec((HIST_ROWS, d), lambda i: (0, 0)))
        conv_rows = tile + HIST_ROWS
    else:
        ret_spec = pl.BlockSpec((n_chunks, HEADS, HEAD_DIM, HEAD_DIM), lambda i: (i, 0, 0, 0))
        conv_spec = pl.BlockSpec((n_chunks, HIST_ROWS, d), lambda i: (i, 0, 0))
        inputs = [x, state_ret, state_conv] + weights
        in_specs = [tok, ret_spec, conv_spec] + w_specs
        out_shape = (jax.ShapeDtypeStruct((n, d), F32),
                     jax.ShapeDtypeStruct(state_ret.shape, F32),
                     jax.ShapeDtypeStruct(state_conv.shape, F32))
        out_specs = (tok, ret_spec, conv_spec)
        conv_rows = CHUNK + HIST_ROWS
    scratch = [
        pltpu.VMEM((HEADS, tile, tile), F32),
        pltpu.VMEM((HEADS, tile, HEAD_DIM), F32),
        pltpu.VMEM((HEADS, tile, HEAD_DIM), F32),
        pltpu.VMEM((tile, d), BF16),
        pltpu.VMEM((tile, d), BF16),
        pltpu.VMEM((tile, d), BF16),
        pltpu.VMEM((tile, d), BF16),
        pltpu.VMEM((tile, d), BF16),
        pltpu.VMEM((tile, d), F32),
        pltpu.VMEM((tile, d), BF16),
        pltpu.VMEM((conv_rows, d), F32),
    ]
    return pl.pallas_call(
        functools.partial(_mixer_kernel, carry, tile),
        out_shape=out_shape,
        grid=grid,
        in_specs=in_specs,
        out_specs=out_specs,
        scratch_shapes=scratch,
        compiler_params=pltpu.CompilerParams(
            dimension_semantics=("arbitrary",), vmem_limit_bytes=VMEM_LIMIT_BYTES),
        name="mixer_prompt" if carry else "mixer_sample",
    )(*inputs)


def _mlp_kernel(h_ref, n2_ref, w1_ref, w2_ref, nf_ref, y_ref):
    h = h_ref[...]
    hn = _rms(h, n2_ref[...]).astype(BF16)
    a = jnp.maximum(_dot(hn, w1_ref[...]), 0.0)
    o = h + _dot((a * a).astype(BF16), w2_ref[...])
    y_ref[...] = _rms(o, nf_ref[...])


def _mlp(h, n2, w1, w2, nf, name):
    n, d = h.shape
    tile = MLP_TILE
    tok = pl.BlockSpec((tile, d), lambda i: (i, 0))
    weights = [n2, w1, w2, nf]
    return pl.pallas_call(
        _mlp_kernel,
        out_shape=jax.ShapeDtypeStruct((n, d), F32),
        grid=(n // tile,),
        in_specs=[tok] + [_resident(w.shape) for w in weights],
        out_specs=tok,
        compiler_params=pltpu.CompilerParams(
            dimension_semantics=("arbitrary",), vmem_limit_bytes=VMEM_LIMIT_BYTES),
        name=name,
    )(h, *weights)


def kernel(x_prompt, x_sample, state_ret, state_conv, norm1, w_in, ret_gn_w, conv_w, w_ret_out, w_conv_out, w_o, norm2, w_ff1, w_ff2, norm_f):
    depth = w_in.shape[0]
    assert depth == 1, "single-layer kernel"
    bp, lp, d = x_prompt.shape
    bs, ls, _ = x_sample.shape
    assert bp == 1 and ls == CHUNK and lp % MIXER_TILE == 0 and (bs * ls) % MIXER_TILE == 0
    assert lp % MLP_TILE == 0 and (bs * ls) % MLP_TILE == 0

    row = lambda v: v.reshape(1, -1)
    mixer_w = (row(norm1[0]), w_in[0].astype(BF16), row(ret_gn_w[0]), conv_w[0],
               w_ret_out[0].astype(BF16), w_conv_out[0].astype(BF16), w_o[0].astype(BF16))
    mlp_w = (row(norm2[0]), w_ff1[0].astype(BF16), w_ff2[0].astype(BF16), row(norm_f))

    hp, ret_p, conv_p = _mixer(x_prompt.reshape(lp, d), None, None, *mixer_w, carry=True)
    conv_hist = jnp.pad(state_conv[0], ((0, 0), (HIST_ROWS - (CONV_W - 1), 0), (0, 0)))
    hs, ret_s, conv_s = _mixer(x_sample.reshape(bs * ls, d), state_ret[0], conv_hist, *mixer_w, carry=False)

    yp = _mlp(hp, *mlp_w, name="mlp_prompt").reshape(bp, lp, d)
    ys = _mlp(hs, *mlp_w, name="mlp_sample").reshape(bs, ls, d)
    keep = slice(HIST_ROWS - (CONV_W - 1), HIST_ROWS)
    return (yp, ys, ret_p[None, None], conv_p[None, None, keep],
            ret_s[None], conv_s[None, :, keep])
```

```python
import functools
import math

import jax
import jax.numpy as jnp
from jax import lax
from jax.experimental import pallas as pl
from jax.experimental.pallas import tpu as pltpu

CHUNK = 64
CHUNK_SHIFT = 6
HEADS = 8
HEAD_DIM = 128
CONV_W = 3
ROPE_BASE = 10000.0
EPS = 1e-6
PAST_LEN = 4096
HIST_ROWS = 8

MIXER_TILE = 256
MLP_TILE = 512
VMEM_LIMIT_BYTES = 56 * 1024 * 1024

F32 = jnp.float32
BF16 = jnp.bfloat16


def _log_gamma(h):
    return math.log(1.0 - 2.0 ** (-5.0 - h))


def _rms(x, w):
    ms = jnp.mean(x * x, axis=-1, keepdims=True)
    return (x * lax.rsqrt(ms + EPS)) * w


def _dot(a, b):
    return jnp.dot(a, b, preferred_element_type=F32)


def _dot_nt(a, b):
    return lax.dot_general(a, b, (((1,), (1,)), ((), ())), preferred_element_type=F32)


def _dot_tn(a, b):
    return lax.dot_general(a, b, (((0,), (0,)), ((), ())), preferred_element_type=F32)


def _rope_cos_sin(pos):
    lane = lax.broadcasted_iota(jnp.int32, pos.shape, 1)
    half = HEAD_DIM // 2
    inv = jnp.exp((-math.log(ROPE_BASE)) * (lane & (half - 1)).astype(F32) / half)
    ang = pos * inv
    return jnp.cos(ang), jnp.where(lane < half, -1.0, 1.0) * jnp.sin(ang)


def _mixer_kernel(carry, tile, *refs):
    n_chunks = tile // CHUNK
    if carry:
        (x_ref, n1_ref, win_ref, gnw_ref, cw_ref, wro_ref, wco_ref, wo_ref,
         h_ref, sret_ref, sconv_ref,
         dmat, rowdec, coldec, cos_s, sin_s, xn_s, q_s, k_s, kd_s, v_s, sg_s, gated_s, ubuf,
         cosb_s, sinb_s) = refs
        sin_ret_ref = sin_conv_ref = None
    else:
        (x_ref, sin_ret_ref, sin_conv_ref, n1_ref, win_ref, gnw_ref, cw_ref, wro_ref, wco_ref, wo_ref,
         h_ref, sret_ref, sconv_ref,
         dmat, rowdec, coldec, cos_s, sin_s, xn_s, q_s, k_s, kd_s, v_s, sg_s, gated_s, ubuf) = refs
    d_model = x_ref.shape[-1]
    step = pl.program_id(0)
    state_len = tile if carry else CHUNK

    @pl.when(step == 0)
    def _init():
        ii = lax.broadcasted_iota(jnp.int32, (tile, tile), 0)
        jj = lax.broadcasted_iota(jnp.int32, (tile, tile), 1)
        dist = jnp.abs(ii - jj).astype(F32)
        ci, cj = ii >> CHUNK_SHIFT, jj >> CHUNK_SHIFT
        keep = (cj <= ci) if carry else (cj == ci)
        row = lax.broadcasted_iota(jnp.int32, (tile, HEAD_DIM), 0)
        loc = (row if carry else row & (CHUNK - 1)).astype(F32)
        for h in range(HEADS):
            lg = _log_gamma(h)
            dmat[h] = jnp.where(keep, jnp.exp(lg * dist), 0.0)
            rowdec[h] = jnp.exp(lg * (loc + 1.0))
            coldec[h] = jnp.exp(lg * (state_len - 1.0 - loc))
        if carry:
            cosb_s[...], sinb_s[...] = _rope_cos_sin(row.astype(F32))
        else:
            cos_s[...], sin_s[...] = _rope_cos_sin((PAST_LEN + (row & (CHUNK - 1))).astype(F32))
        if carry:
            sret_ref[...] = jnp.zeros_like(sret_ref)
            ubuf[0:HIST_ROWS, :] = jnp.zeros((HIST_ROWS, d_model), F32)

    x = x_ref[...]
    xn_s[...] = _rms(x, n1_ref[...]).astype(BF16)

    def proj(g):
        return _dot(xn_s[...], win_ref[:, g * d_model:(g + 1) * d_model])

    if carry:
        first = jnp.zeros((8, HEAD_DIM), jnp.int32) + step * tile
        ct, st = _rope_cos_sin(first.astype(F32))
        ct, st = ct[0:1, :], st[0:1, :]
        cb, sb = cosb_s[...], sinb_s[...]
        cos_s[...] = cb * ct - sb * st
        sin_s[...] = sb * ct + cb * st

    def rope(t):
        return t * cos_s[...] + pltpu.roll(t, HEAD_DIM // 2, 1) * sin_s[...]

    q = proj(0)
    for h in range(HEADS):
        hs = slice(h * HEAD_DIM, (h + 1) * HEAD_DIM)
        q_s[:, hs] = rope(q[:, hs]).astype(BF16)
    k = proj(1)
    for h in range(HEADS):
        hs = slice(h * HEAD_DIM, (h + 1) * HEAD_DIM)
        kr = rope(k[:, hs]) * (HEAD_DIM ** -0.5)
        k_s[:, hs] = kr.astype(BF16)
        kd_s[:, hs] = (kr * coldec[h]).astype(BF16)
    v_s[...] = proj(2).astype(BF16)
    sg_s[...] = jax.nn.silu(proj(3))

    for h in range(HEADS):
        hs = slice(h * HEAD_DIM, (h + 1) * HEAD_DIM)
        decay = math.exp(_log_gamma(h) * state_len)
        qh, kh, kdh, vh = q_s[:, hs], k_s[:, hs], kd_s[:, hs], v_s[:, hs]
        p = (_dot_nt(qh, kh) * dmat[h]).astype(BF16)
        o = _dot(p, vh)
        if carry:
            s0 = sret_ref[h]
            o = o + _dot(qh, s0.astype(BF16)) * rowdec[h]
            sret_ref[h] = s0 * decay + _dot_tn(kdh, vh)
        else:
            inter = []
            for c in range(n_chunks):
                rs = slice(c * CHUNK, (c + 1) * CHUNK)
                s0 = sin_ret_ref[c, h]
                inter.append(_dot(qh[rs], s0.astype(BF16)))
                sret_ref[c, h] = s0 * decay + _dot_tn(kdh[rs], vh[rs])
            o = o + jnp.concatenate(inter, axis=0) * rowdec[h]
        mu = jnp.mean(o, axis=-1, keepdims=True)
        d = o - mu
        var = jnp.mean(d * d, axis=-1, keepdims=True)
        on = (d * lax.rsqrt(var + EPS)) * gnw_ref[:, hs]
        gated_s[:, hs] = (on * sg_s[:, hs]).astype(BF16)
    ret_y = _dot(gated_s[...], wro_ref[...])
    mix = jax.nn.sigmoid(proj(7)) * ret_y

    u = proj(5) * proj(6)
    w0, w1, w2 = cw_ref[0:1, :], cw_ref[1:2, :], cw_ref[2:3, :]
    if carry:
        ubuf[HIST_ROWS:HIST_ROWS + tile, :] = u
        y = (w0 * ubuf[HIST_ROWS - 2:HIST_ROWS - 2 + tile, :]
             + w1 * ubuf[HIST_ROWS - 1:HIST_ROWS - 1 + tile, :]) + w2 * u
        tail = ubuf[tile:tile + HIST_ROWS, :]
        sconv_ref[...] = tail
        ubuf[0:HIST_ROWS, :] = tail
    else:
        ys = []
        for c in range(n_chunks):
            ubuf[0:HIST_ROWS, :] = sin_conv_ref[c]
            ubuf[HIST_ROWS:HIST_ROWS + CHUNK, :] = u[c * CHUNK:(c + 1) * CHUNK]
            ys.append((w0 * ubuf[HIST_ROWS - 2:HIST_ROWS - 2 + CHUNK, :]
                       + w1 * ubuf[HIST_ROWS - 1:HIST_ROWS - 1 + CHUNK, :])
                      + w2 * ubuf[HIST_ROWS:HIST_ROWS + CHUNK, :])
            sconv_ref[c] = ubuf[CHUNK:CHUNK + HIST_ROWS, :]
        y = jnp.concatenate(ys, axis=0)
    conv_y = _dot((proj(4) * y).astype(BF16), wco_ref[...])
    mix = mix + jax.nn.sigmoid(proj(8)) * conv_y

    h_ref[...] = x + _dot(mix.astype(BF16), wo_ref[...])


def _resident(shape):
    nd = len(shape)
    return pl.BlockSpec(shape, lambda i: (0,) * nd, pipeline_mode=pl.Buffered(1))


def _mixer(x, state_ret, state_conv, n1, win, gnw, cw, wro, wco, wo, *, carry):
    n, d = x.shape
    tile = MIXER_TILE
    n_chunks = tile // CHUNK
    grid = (n // tile,)
    tok = pl.BlockSpec((tile, d), lambda i: (i, 0))
    weights = [n1, win, gnw, cw, wro, wco, wo]
    w_specs = [_resident(w.shape) for w in weights]
    if carry:
        inputs = [x] + weights
        in_specs = [tok] + w_specs
        out_shape = (jax.ShapeDtypeStruct((n, d), F32),
                     jax.ShapeDtypeStruct((HEADS, HEAD_DIM, HEAD_DIM), F32),
                     jax.ShapeDtypeStruct((HIST_ROWS, d), F32))
        out_specs = (tok,
                     pl.BlockSpec((HEADS, HEAD_DIM, HEAD_DIM), lambda i: (0, 0, 0)),
                     pl.BlockSpec((HIST_ROWS, d), lambda i: (0, 0)))
        conv_rows = tile + HIST_ROWS
    else:
        ret_spec = pl.BlockSpec((n_chunks, HEADS, HEAD_DIM, HEAD_DIM), lambda i: (i, 0, 0, 0))
        conv_spec = pl.BlockSpec((n_chunks, HIST_ROWS, d), lambda i: (i, 0, 0))
        inputs = [x, state_ret, state_conv] + weights
        in_specs = [tok, ret_spec, conv_spec] + w_specs
        out_shape = (jax.ShapeDtypeStruct((n, d), F32),
                     jax.ShapeDtypeStruct(state_ret.shape, F32),
                     jax.ShapeDtypeStruct(state_conv.shape, F32))
        out_specs = (tok, ret_spec, conv_spec)
        conv_rows = CHUNK + HIST_ROWS
    scratch = [
        pltpu.VMEM((HEADS, tile, tile), F32),
        pltpu.VMEM((HEADS, tile, HEAD_DIM), F32),
        pltpu.VMEM((HEADS, tile, HEAD_DIM), F32),
        pltpu.VMEM((tile, HEAD_DIM), F32),
        pltpu.VMEM((tile, HEAD_DIM), F32),
        pltpu.VMEM((tile, d), BF16),
        pltpu.VMEM((tile, d), BF16),
        pltpu.VMEM((tile, d), BF16),
        pltpu.VMEM((tile, d), BF16),
        pltpu.VMEM((tile, d), BF16),
        pltpu.VMEM((tile, d), F32),
        pltpu.VMEM((tile, d), BF16),
        pltpu.VMEM((conv_rows, d), F32),
    ]
    if carry:
        scratch += [pltpu.VMEM((tile, HEAD_DIM), F32)] * 2
    return pl.pallas_call(
        functools.partial(_mixer_kernel, carry, tile),
        out_shape=out_shape,
        grid=grid,
        in_specs=in_specs,
        out_specs=out_specs,
        scratch_shapes=scratch,
        compiler_params=pltpu.CompilerParams(
            dimension_semantics=("arbitrary",), vmem_limit_bytes=VMEM_LIMIT_BYTES),
        name="mixer_prompt" if carry else "mixer_sample",
    )(*inputs)


def _mlp_kernel(h_ref, n2_ref, w1_ref, w2_ref, nf_ref, y_ref):
    h = h_ref[...]
    hn = _rms(h, n2_ref[...]).astype(BF16)
    a = jnp.maximum(_dot(hn, w1_ref[...]), 0.0)
    o = h + _dot((a * a).astype(BF16), w2_ref[...])
    y_ref[...] = _rms(o, nf_ref[...])


def _mlp(h, n2, w1, w2, nf, name):
    n, d = h.shape
    tile = MLP_TILE
    tok = pl.BlockSpec((tile, d), lambda i: (i, 0))
    weights = [n2, w1, w2, nf]
    return pl.pallas_call(
        _mlp_kernel,
        out_shape=jax.ShapeDtypeStruct((n, d), F32),
        grid=(n // tile,),
        in_specs=[tok] + [_resident(w.shape) for w in weights],
        out_specs=tok,
        compiler_params=pltpu.CompilerParams(
            dimension_semantics=("arbitrary",), vmem_limit_bytes=VMEM_LIMIT_BYTES),
        name=name,
    )(h, *weights)


def kernel(x_prompt, x_sample, state_ret, state_conv, norm1, w_in, ret_gn_w, conv_w, w_ret_out, w_conv_out, w_o, norm2, w_ff1, w_ff2, norm_f):
    depth = w_in.shape[0]
    assert depth == 1, "single-layer kernel"
    bp, lp, d = x_prompt.shape
    bs, ls, _ = x_sample.shape
    assert bp == 1 and ls == CHUNK and lp % MIXER_TILE == 0 and (bs * ls) % MIXER_TILE == 0
    assert lp % MLP_TILE == 0 and (bs * ls) % MLP_TILE == 0

    row = lambda v: v.reshape(1, -1)
    mixer_w = (row(norm1[0]), w_in[0].astype(BF16), row(ret_gn_w[0]), conv_w[0],
               w_ret_out[0].astype(BF16), w_conv_out[0].astype(BF16), w_o[0].astype(BF16))
    mlp_w = (row(norm2[0]), w_ff1[0].astype(BF16), w_ff2[0].astype(BF16), row(norm_f))

    hp, ret_p, conv_p = _mixer(x_prompt.reshape(lp, d), None, None, *mixer_w, carry=True)
    conv_hist = jnp.pad(state_conv[0], ((0, 0), (HIST_ROWS - (CONV_W - 1), 0), (0, 0)))
    hs, ret_s, conv_s = _mixer(x_sample.reshape(bs * ls, d), state_ret[0], conv_hist, *mixer_w, carry=False)

    yp = _mlp(hp, *mlp_w, name="mlp_prompt").reshape(bp, lp, d)
    ys = _mlp(hs, *mlp_w, name="mlp_sample").reshape(bs, ls, d)
    keep = slice(HIST_ROWS - (CONV_W - 1), HIST_ROWS)
    return (yp, ys, ret_p[None, None], conv_p[None, None, keep],
            ret_s[None], conv_s[None, :, keep])
```

```python
import functools
import math

import jax
import jax.numpy as jnp
from jax import lax
from jax.experimental import pallas as pl
from jax.experimental.pallas import tpu as pltpu

CHUNK = 64
CHUNK_SHIFT = 6
HEADS = 8
HEAD_DIM = 128
CONV_W = 3
ROPE_BASE = 10000.0
EPS = 1e-6
PAST_LEN = 4096
HIST_ROWS = 8

MIXER_TILE = 256
PROMPT_SUBTILES = 2
SAMPLE_SUBTILES = 1
MLP_TILE = 512
MLP_SUBTILES = 2
VMEM_LIMIT_BYTES = 56 * 1024 * 1024

F32 = jnp.float32
BF16 = jnp.bfloat16


def _log_gamma(h):
    return math.log(1.0 - 2.0 ** (-5.0 - h))


def _rms(x, w):
    ms = jnp.mean(x * x, axis=-1, keepdims=True)
    return (x * lax.rsqrt(ms + EPS)) * w


def _dot(a, b):
    return jnp.dot(a, b, preferred_element_type=F32)


def _dot_nt(a, b):
    return lax.dot_general(a, b, (((1,), (1,)), ((), ())), preferred_element_type=F32)


def _dot_tn(a, b):
    return lax.dot_general(a, b, (((0,), (0,)), ((), ())), preferred_element_type=F32)


def _rope_cos_sin(pos):
    lane = lax.broadcasted_iota(jnp.int32, pos.shape, 1)
    half = HEAD_DIM // 2
    inv = jnp.exp((-math.log(ROPE_BASE)) * (lane & (half - 1)).astype(F32) / half)
    ang = pos * inv
    return jnp.cos(ang), jnp.where(lane < half, -1.0, 1.0) * jnp.sin(ang)


def _mixer_kernel(carry, tile, subtiles, *refs):
    n_chunks = tile // CHUNK
    if carry:
        (x_ref, n1_ref, win_ref, gnw_ref, cw_ref, wro_ref, wco_ref, wo_ref,
         h_ref, sret_ref, sconv_ref,
         dmat, rowdec, coldec, cos_s, sin_s, xn_s, q_s, k_s, kd_s, v_s, sg_s, gated_s, ubuf,
         cosb_s, sinb_s) = refs
        sin_ret_ref = sin_conv_ref = None
    else:
        (x_ref, sin_ret_ref, sin_conv_ref, n1_ref, win_ref, gnw_ref, cw_ref, wro_ref, wco_ref, wo_ref,
         h_ref, sret_ref, sconv_ref,
         dmat, rowdec, coldec, cos_s, sin_s, xn_s, q_s, k_s, kd_s, v_s, sg_s, gated_s, ubuf) = refs
    d_model = x_ref.shape[-1]
    step = pl.program_id(0)
    state_len = tile if carry else CHUNK

    @pl.when(step == 0)
    def _init():
        ii = lax.broadcasted_iota(jnp.int32, (tile, tile), 0)
        jj = lax.broadcasted_iota(jnp.int32, (tile, tile), 1)
        dist = jnp.abs(ii - jj).astype(F32)
        ci, cj = ii >> CHUNK_SHIFT, jj >> CHUNK_SHIFT
        keep = (cj <= ci) if carry else (cj == ci)
        row = lax.broadcasted_iota(jnp.int32, (tile, HEAD_DIM), 0)
        loc = (row if carry else row & (CHUNK - 1)).astype(F32)
        for h in range(HEADS):
            lg = _log_gamma(h)
            dmat[h] = jnp.where(keep, jnp.exp(lg * dist), 0.0)
            rowdec[h] = jnp.exp(lg * (loc + 1.0))
            coldec[h] = jnp.exp(lg * (state_len - 1.0 - loc))
        if carry:
            cosb_s[...], sinb_s[...] = _rope_cos_sin(row.astype(F32))
            sret_ref[...] = jnp.zeros_like(sret_ref)
            ubuf[0:HIST_ROWS, :] = jnp.zeros((HIST_ROWS, d_model), F32)
        else:
            cos, sin = _rope_cos_sin((PAST_LEN + (row & (CHUNK - 1))).astype(F32))
            for sub in range(subtiles):
                cos_s[sub], sin_s[sub] = cos, sin

    w0, w1, w2 = cw_ref[0:1, :], cw_ref[1:2, :], cw_ref[2:3, :]

    for sub in range(subtiles):
        rows = slice(sub * tile, (sub + 1) * tile)
        x = x_ref[rows, :]
        xn_s[sub] = _rms(x, n1_ref[...]).astype(BF16)

        def proj(g):
            return _dot(xn_s[sub], win_ref[:, g * d_model:(g + 1) * d_model])

        if carry:
            first = jnp.zeros((8, HEAD_DIM), jnp.int32) + (step * subtiles + sub) * tile
            ct, st = _rope_cos_sin(first.astype(F32))
            ct, st = ct[0:1, :], st[0:1, :]
            cb, sb = cosb_s[...], sinb_s[...]
            cos_s[sub] = cb * ct - sb * st
            sin_s[sub] = sb * ct + cb * st

        def rope(t):
            return t * cos_s[sub] + pltpu.roll(t, HEAD_DIM // 2, 1) * sin_s[sub]

        q = proj(0)
        for h in range(HEADS):
            hs = slice(h * HEAD_DIM, (h + 1) * HEAD_DIM)
            q_s[sub, :, hs] = rope(q[:, hs]).astype(BF16)
        k = proj(1)
        for h in range(HEADS):
            hs = slice(h * HEAD_DIM, (h + 1) * HEAD_DIM)
            kr = rope(k[:, hs]) * (HEAD_DIM ** -0.5)
            k_s[sub, :, hs] = kr.astype(BF16)
            kd_s[sub, :, hs] = (kr * coldec[h]).astype(BF16)
        v_s[sub] = proj(2).astype(BF16)
        sg_s[sub] = jax.nn.silu(proj(3))

        for h in range(HEADS):
            hs = slice(h * HEAD_DIM, (h + 1) * HEAD_DIM)
            decay = math.exp(_log_gamma(h) * state_len)
            qh, kh, kdh, vh = q_s[sub, :, hs], k_s[sub, :, hs], kd_s[sub, :, hs], v_s[sub, :, hs]
            p = (_dot_nt(qh, kh) * dmat[h]).astype(BF16)
            o = _dot(p, vh)
            if carry:
                s0 = sret_ref[h]
                o = o + _dot(qh, s0.astype(BF16)) * rowdec[h]
                sret_ref[h] = s0 * decay + _dot_tn(kdh, vh)
            else:
                inter = []
                for c in range(n_chunks):
                    rs = slice(c * CHUNK, (c + 1) * CHUNK)
                    seq = sub * n_chunks + c
                    s0 = sin_ret_ref[seq, h]
                    inter.append(_dot(qh[rs], s0.astype(BF16)))
                    sret_ref[seq, h] = s0 * decay + _dot_tn(kdh[rs], vh[rs])
                o = o + jnp.concatenate(inter, axis=0) * rowdec[h]
            mu = jnp.mean(o, axis=-1, keepdims=True)
            d = o - mu
            var = jnp.mean(d * d, axis=-1, keepdims=True)
            on = (d * lax.rsqrt(var + EPS)) * gnw_ref[:, hs]
            gated_s[sub, :, hs] = (on * sg_s[sub, :, hs]).astype(BF16)
        ret_y = _dot(gated_s[sub], wro_ref[...])
        mix = jax.nn.sigmoid(proj(7)) * ret_y

        u = proj(5) * proj(6)
        if carry:
            base = HIST_ROWS + sub * tile
            ubuf[base:base + tile, :] = u
            y = (w0 * ubuf[base - 2:base - 2 + tile, :]
                 + w1 * ubuf[base - 1:base - 1 + tile, :]) + w2 * u
        else:
            ys = []
            for c in range(n_chunks):
                seq = sub * n_chunks + c
                base = seq * (CHUNK + HIST_ROWS) + HIST_ROWS
                ubuf[base - HIST_ROWS:base, :] = sin_conv_ref[seq]
                ubuf[base:base + CHUNK, :] = u[c * CHUNK:(c + 1) * CHUNK]
                ys.append((w0 * ubuf[base - 2:base - 2 + CHUNK, :]
                           + w1 * ubuf[base - 1:base - 1 + CHUNK, :])
                          + w2 * ubuf[base:base + CHUNK, :])
                sconv_ref[seq] = ubuf[base + CHUNK - HIST_ROWS:base + CHUNK, :]
            y = jnp.concatenate(ys, axis=0)
        conv_y = _dot((proj(4) * y).astype(BF16), wco_ref[...])
        mix = mix + jax.nn.sigmoid(proj(8)) * conv_y

        h_ref[rows, :] = x + _dot(mix.astype(BF16), wo_ref[...])

    if carry:
        tail = ubuf[subtiles * tile:subtiles * tile + HIST_ROWS, :]
        sconv_ref[...] = tail
        ubuf[0:HIST_ROWS, :] = tail


def _resident(shape):
    nd = len(shape)
    return pl.BlockSpec(shape, lambda i: (0,) * nd, pipeline_mode=pl.Buffered(1))


def _mixer(x, state_ret, state_conv, n1, win, gnw, cw, wro, wco, wo, *, carry):
    n, d = x.shape
    tile = MIXER_TILE
    subtiles = PROMPT_SUBTILES if carry else SAMPLE_SUBTILES
    rows = tile * subtiles
    n_seqs = rows // CHUNK
    grid = (n // rows,)
    tok = pl.BlockSpec((rows, d), lambda i: (i, 0))
    weights = [n1, win, gnw, cw, wro, wco, wo]
    w_specs = [_resident(w.shape) for w in weights]
    if carry:
        inputs = [x] + weights
        in_specs = [tok] + w_specs
        out_shape = (jax.ShapeDtypeStruct((n, d), F32),
                     jax.ShapeDtypeStruct((HEADS, HEAD_DIM, HEAD_DIM), F32),
                     jax.ShapeDtypeStruct((HIST_ROWS, d), F32))
        out_specs = (tok,
                     pl.BlockSpec((HEADS, HEAD_DIM, HEAD_DIM), lambda i: (0, 0, 0)),
                     pl.BlockSpec((HIST_ROWS, d), lambda i: (0, 0)))
        conv_rows = rows + HIST_ROWS
    else:
        ret_spec = pl.BlockSpec((n_seqs, HEADS, HEAD_DIM, HEAD_DIM), lambda i: (i, 0, 0, 0))
        conv_spec = pl.BlockSpec((n_seqs, HIST_ROWS, d), lambda i: (i, 0, 0))
        inputs = [x, state_ret, state_conv] + weights
        in_specs = [tok, ret_spec, conv_spec] + w_specs
        out_shape = (jax.ShapeDtypeStruct((n, d), F32),
                     jax.ShapeDtypeStruct(state_ret.shape, F32),
                     jax.ShapeDtypeStruct(state_conv.shape, F32))
        out_specs = (tok, ret_spec, conv_spec)
        conv_rows = n_seqs * (CHUNK + HIST_ROWS)
    scratch = [
        pltpu.VMEM((HEADS, tile, tile), F32),
        pltpu.VMEM((HEADS, tile, HEAD_DIM), F32),
        pltpu.VMEM((HEADS, tile, HEAD_DIM), F32),
        pltpu.VMEM((subtiles, tile, HEAD_DIM), F32),
        pltpu.VMEM((subtiles, tile, HEAD_DIM), F32),
        pltpu.VMEM((subtiles, tile, d), BF16),
        pltpu.VMEM((subtiles, tile, d), BF16),
        pltpu.VMEM((subtiles, tile, d), BF16),
        pltpu.VMEM((subtiles, tile, d), BF16),
        pltpu.VMEM((subtiles, tile, d), BF16),
        pltpu.VMEM((subtiles, tile, d), F32),
        pltpu.VMEM((subtiles, tile, d), BF16),
        pltpu.VMEM((conv_rows, d), F32),
    ]
    if carry:
        scratch += [pltpu.VMEM((tile, HEAD_DIM), F32)] * 2
    return pl.pallas_call(
        functools.partial(_mixer_kernel, carry, tile, subtiles),
        out_shape=out_shape,
        grid=grid,
        in_specs=in_specs,
        out_specs=out_specs,
        scratch_shapes=scratch,
        compiler_params=pltpu.CompilerParams(
            dimension_semantics=("arbitrary",), vmem_limit_bytes=VMEM_LIMIT_BYTES),
        name="mixer_prompt" if carry else "mixer_sample",
    )(*inputs)


def _mlp_kernel(tile, subtiles, h_ref, n2_ref, w1_ref, w2_ref, nf_ref, y_ref):
    for sub in range(subtiles):
        rows = slice(sub * tile, (sub + 1) * tile)
        h = h_ref[rows, :]
        hn = _rms(h, n2_ref[...]).astype(BF16)
        a = jnp.maximum(_dot(hn, w1_ref[...]), 0.0)
        o = h + _dot((a * a).astype(BF16), w2_ref[...])
        y_ref[rows, :] = _rms(o, nf_ref[...])


def _mlp(h, n2, w1, w2, nf, name):
    n, d = h.shape
    tile, subtiles = MLP_TILE, MLP_SUBTILES
    tok = pl.BlockSpec((tile * subtiles, d), lambda i: (i, 0))
    weights = [n2, w1, w2, nf]
    return pl.pallas_call(
        functools.partial(_mlp_kernel, tile, subtiles),
        out_shape=jax.ShapeDtypeStruct((n, d), F32),
        grid=(n // (tile * subtiles),),
        in_specs=[tok] + [_resident(w.shape) for w in weights],
        out_specs=tok,
        compiler_params=pltpu.CompilerParams(
            dimension_semantics=("arbitrary",), vmem_limit_bytes=VMEM_LIMIT_BYTES),
        name=name,
    )(h, *weights)


def kernel(x_prompt, x_sample, state_ret, state_conv, norm1, w_in, ret_gn_w, conv_w, w_ret_out, w_conv_out, w_o, norm2, w_ff1, w_ff2, norm_f):
    depth = w_in.shape[0]
    assert depth == 1, "single-layer kernel"
    bp, lp, d = x_prompt.shape
    bs, ls, _ = x_sample.shape
    assert bp == 1 and ls == CHUNK
    assert lp % (MIXER_TILE * PROMPT_SUBTILES) == 0 and (bs * ls) % (MIXER_TILE * SAMPLE_SUBTILES) == 0
    assert lp % (MLP_TILE * MLP_SUBTILES) == 0 and (bs * ls) % (MLP_TILE * MLP_SUBTILES) == 0

    row = lambda v: v.reshape(1, -1)
    mixer_w = (row(norm1[0]), w_in[0].astype(BF16), row(ret_gn_w[0]), conv_w[0],
               w_ret_out[0].astype(BF16), w_conv_out[0].astype(BF16), w_o[0].astype(BF16))
    mlp_w = (row(norm2[0]), w_ff1[0].astype(BF16), w_ff2[0].astype(BF16), row(norm_f))

    hp, ret_p, conv_p = _mixer(x_prompt.reshape(lp, d), None, None, *mixer_w, carry=True)
    conv_hist = jnp.pad(state_conv[0], ((0, 0), (HIST_ROWS - (CONV_W - 1), 0), (0, 0)))
    hs, ret_s, conv_s = _mixer(x_sample.reshape(bs * ls, d), state_ret[0], conv_hist, *mixer_w, carry=False)

    yp = _mlp(hp, *mlp_w, name="mlp_prompt").reshape(bp, lp, d)
    ys = _mlp(hs, *mlp_w, name="mlp_sample").reshape(bs, ls, d)
    keep = slice(HIST_ROWS - (CONV_W - 1), HIST_ROWS)
    return (yp, ys, ret_p[None, None], conv_p[None, None, keep],
            ret_s[None], conv_s[None, :, keep])
```

```python
import functools
import math

import jax
import jax.numpy as jnp
from jax import lax
from jax.experimental import pallas as pl
from jax.experimental.pallas import tpu as pltpu

CHUNK = 64
CHUNK_SHIFT = 6
HEADS = 8
HEAD_DIM = 128
CONV_W = 3
ROPE_BASE = 10000.0
EPS = 1e-6
PAST_LEN = 4096
HIST_ROWS = 8
BF16_SUBLANES = 16

MIXER_TILE = 256
PROMPT_SUBTILES = 2
SAMPLE_SUBTILES = 1
MLP_TILE = 512
MLP_SUBTILES = 2
VMEM_LIMIT_BYTES = 56 * 1024 * 1024

F32 = jnp.float32
BF16 = jnp.bfloat16


def _log_gamma(h):
    return math.log(1.0 - 2.0 ** (-5.0 - h))


def _rms(x, w):
    ms = jnp.mean(x * x, axis=-1, keepdims=True)
    return (x * lax.rsqrt(ms + EPS)) * w


def _dot(a, b):
    return jnp.dot(a, b, preferred_element_type=F32)


def _dot_nt(a, b):
    return lax.dot_general(a, b, (((1,), (1,)), ((), ())), preferred_element_type=F32)


def _dot_tn(a, b):
    return lax.dot_general(a, b, (((0,), (0,)), ((), ())), preferred_element_type=F32)


def _rope_cos_sin(pos):
    lane = lax.broadcasted_iota(jnp.int32, pos.shape, 1)
    half = HEAD_DIM // 2
    inv = jnp.exp((-math.log(ROPE_BASE)) * (lane & (half - 1)).astype(F32) / half)
    ang = pos * inv
    return jnp.cos(ang), jnp.where(lane < half, -1.0, 1.0) * jnp.sin(ang)


def _mixer_kernel(carry, tile, subtiles, *refs):
    n_chunks = tile // CHUNK
    if carry:
        (x_ref, n1_ref, win_ref, gnw_ref, cw_ref, wro_ref, wco_ref, wo_ref, ff1_ref, ff2_ref,
         h_ref, sret_ref, sconv_ref, ff1_bf_ref, ff2_bf_ref,
         dmat, rowdec, coldec, cos_s, sin_s, xn_s, q_s, k_s, kd_s, v_s, sg_s, gated_s, ubuf,
         cosb_s, sinb_s) = refs
        sin_ret_ref = sin_conv_ref = None
        ff1_bf_ref[...] = ff1_ref[...].astype(BF16)
        ff2_bf_ref[...] = ff2_ref[...].astype(BF16)
    else:
        (x_ref, sin_ret_ref, sin_conv_ref, n1_ref, win_ref, gnw_ref, cw_ref, wro_ref, wco_ref, wo_ref,
         h_ref, sret_ref, sconv_ref,
         dmat, rowdec, coldec, cos_s, sin_s, xn_s, q_s, k_s, kd_s, v_s, sg_s, gated_s, ubuf) = refs
    d_model = x_ref.shape[-1]
    step = pl.program_id(0)
    state_len = tile if carry else CHUNK

    @pl.when(step == 0)
    def _init():
        ii = lax.broadcasted_iota(jnp.int32, (tile, tile), 0)
        jj = lax.broadcasted_iota(jnp.int32, (tile, tile), 1)
        dist = jnp.abs(ii - jj).astype(F32)
        ci, cj = ii >> CHUNK_SHIFT, jj >> CHUNK_SHIFT
        keep = (cj <= ci) if carry else (cj == ci)
        row = lax.broadcasted_iota(jnp.int32, (tile, HEAD_DIM), 0)
        loc = (row if carry else row & (CHUNK - 1)).astype(F32)
        for h in range(HEADS):
            lg = _log_gamma(h)
            dmat[h] = jnp.where(keep, jnp.exp(lg * dist), 0.0)
            rowdec[h] = jnp.exp(lg * (loc + 1.0))
            coldec[h] = jnp.exp(lg * (state_len - 1.0 - loc))
        if carry:
            cosb_s[...], sinb_s[...] = _rope_cos_sin(row.astype(F32))
            sret_ref[...] = jnp.zeros_like(sret_ref)
            ubuf[0:HIST_ROWS, :] = jnp.zeros((HIST_ROWS, d_model), F32)
        else:
            cos, sin = _rope_cos_sin((PAST_LEN + (row & (CHUNK - 1))).astype(F32))
            for sub in range(subtiles):
                cos_s[sub], sin_s[sub] = cos, sin

    w0, w1, w2 = cw_ref[0:1, :], cw_ref[1:2, :], cw_ref[2:3, :]

    for sub in range(subtiles):
        rows = slice(sub * tile, (sub + 1) * tile)
        x = x_ref[rows, :]
        xn_s[sub] = _rms(x, n1_ref[...]).astype(BF16)

        def proj(g):
            return _dot(xn_s[sub], win_ref[:, g * d_model:(g + 1) * d_model])

        if carry:
            first = jnp.zeros((8, HEAD_DIM), jnp.int32) + (step * subtiles + sub) * tile
            ct, st = _rope_cos_sin(first.astype(F32))
            ct, st = ct[0:1, :], st[0:1, :]
            cb, sb = cosb_s[...], sinb_s[...]
            cos_s[sub] = cb * ct - sb * st
            sin_s[sub] = sb * ct + cb * st

        def rope(t):
            return t * cos_s[sub] + pltpu.roll(t, HEAD_DIM // 2, 1) * sin_s[sub]

        q = proj(0)
        for h in range(HEADS):
            hs = slice(h * HEAD_DIM, (h + 1) * HEAD_DIM)
            q_s[sub, :, hs] = rope(q[:, hs]).astype(BF16)
        k = proj(1)
        for h in range(HEADS):
            hs = slice(h * HEAD_DIM, (h + 1) * HEAD_DIM)
            kr = rope(k[:, hs]) * (HEAD_DIM ** -0.5)
            k_s[sub, :, hs] = kr.astype(BF16)
            kd_s[sub, :, hs] = (kr * coldec[h]).astype(BF16)
        v_s[sub] = proj(2).astype(BF16)
        sg_s[sub] = jax.nn.silu(proj(3))

        for h in range(HEADS):
            hs = slice(h * HEAD_DIM, (h + 1) * HEAD_DIM)
            decay = math.exp(_log_gamma(h) * state_len)
            qh, kh, kdh, vh = q_s[sub, :, hs], k_s[sub, :, hs], kd_s[sub, :, hs], v_s[sub, :, hs]
            p = (_dot_nt(qh, kh) * dmat[h]).astype(BF16)
            o = _dot(p, vh)
            if carry:
                s0 = sret_ref[h]
                o = o + _dot(qh, s0.astype(BF16)) * rowdec[h]
                sret_ref[h] = s0 * decay + _dot_tn(kdh, vh)
            else:
                inter = []
                for c in range(n_chunks):
                    rs = slice(c * CHUNK, (c + 1) * CHUNK)
                    seq = sub * n_chunks + c
                    s0 = sin_ret_ref[seq, h]
                    inter.append(_dot(qh[rs], s0.astype(BF16)))
                    sret_ref[seq, h] = s0 * decay + _dot_tn(kdh[rs], vh[rs])
                o = o + jnp.concatenate(inter, axis=0) * rowdec[h]
            mu = jnp.mean(o, axis=-1, keepdims=True)
            d = o - mu
            var = jnp.mean(d * d, axis=-1, keepdims=True)
            on = (d * lax.rsqrt(var + EPS)) * gnw_ref[:, hs]
            gated_s[sub, :, hs] = (on * sg_s[sub, :, hs]).astype(BF16)
        ret_y = _dot(gated_s[sub], wro_ref[...])
        mix = jax.nn.sigmoid(proj(7)) * ret_y

        u = proj(5) * proj(6)
        if carry:
            base = HIST_ROWS + sub * tile
            ubuf[base:base + tile, :] = u
            y = (w0 * ubuf[base - 2:base - 2 + tile, :]
                 + w1 * ubuf[base - 1:base - 1 + tile, :]) + w2 * u
        else:
            ys = []
            for c in range(n_chunks):
                seq = sub * n_chunks + c
                base = seq * (CHUNK + HIST_ROWS) + HIST_ROWS
                ubuf[base - HIST_ROWS:base, :] = sin_conv_ref[seq]
                ubuf[base:base + CHUNK, :] = u[c * CHUNK:(c + 1) * CHUNK]
                ys.append((w0 * ubuf[base - 2:base - 2 + CHUNK, :]
                           + w1 * ubuf[base - 1:base - 1 + CHUNK, :])
                          + w2 * ubuf[base:base + CHUNK, :])
                sconv_ref[seq] = ubuf[base + CHUNK - HIST_ROWS:base + CHUNK, :]
            y = jnp.concatenate(ys, axis=0)
        conv_y = _dot((proj(4) * y).astype(BF16), wco_ref[...])
        mix = mix + jax.nn.sigmoid(proj(8)) * conv_y

        h_ref[rows, :] = x + _dot(mix.astype(BF16), wo_ref[...])

    if carry:
        tail = ubuf[subtiles * tile:subtiles * tile + HIST_ROWS, :]
        sconv_ref[...] = tail
        ubuf[0:HIST_ROWS, :] = tail


def _resident(shape):
    nd = len(shape)
    return pl.BlockSpec(shape, lambda i: (0,) * nd, pipeline_mode=pl.Buffered(1))


def _mixer(x, state_ret, state_conv, n1, win, gnw, cw, wro, wco, wo, ff=None, *, carry):
    n, d = x.shape
    tile = MIXER_TILE
    subtiles = PROMPT_SUBTILES if carry else SAMPLE_SUBTILES
    rows = tile * subtiles
    n_seqs = rows // CHUNK
    steps = n // rows
    grid = (steps,)
    tok = pl.BlockSpec((rows, d), lambda i: (i, 0))
    weights = [n1, win, gnw, cw, wro, wco, wo]
    w_specs = [_resident(w.shape) for w in weights]
    if carry:
        slab_specs = []
        for w in ff:
            slab = w.shape[0] // steps
            assert slab * steps == w.shape[0] and slab % BF16_SUBLANES == 0
            slab_specs.append(pl.BlockSpec((slab, w.shape[1]), lambda i: (i, 0)))
        inputs = [x] + weights + list(ff)
        in_specs = [tok] + w_specs + slab_specs
        out_shape = (jax.ShapeDtypeStruct((n, d), F32),
                     jax.ShapeDtypeStruct((HEADS, HEAD_DIM, HEAD_DIM), F32),
                     jax.ShapeDtypeStruct((HIST_ROWS, d), F32),
                     *[jax.ShapeDtypeStruct(w.shape, BF16) for w in ff])
        out_specs = (tok,
                     pl.BlockSpec((HEADS, HEAD_DIM, HEAD_DIM), lambda i: (0, 0, 0)),
                     pl.BlockSpec((HIST_ROWS, d), lambda i: (0, 0)),
                     *slab_specs)
        conv_rows = rows + HIST_ROWS
    else:
        ret_spec = pl.BlockSpec((n_seqs, HEADS, HEAD_DIM, HEAD_DIM), lambda i: (i, 0, 0, 0))
        conv_spec = pl.BlockSpec((n_seqs, HIST_ROWS, d), lambda i: (i, 0, 0))
        inputs = [x, state_ret, state_conv] + weights
        in_specs = [tok, ret_spec, conv_spec] + w_specs
        out_shape = (jax.ShapeDtypeStruct((n, d), F32),
                     jax.ShapeDtypeStruct(state_ret.shape, F32),
                     jax.ShapeDtypeStruct(state_conv.shape, F32))
        out_specs = (tok, ret_spec, conv_spec)
        conv_rows = n_seqs * (CHUNK + HIST_ROWS)
    scratch = [
        pltpu.VMEM((HEADS, tile, tile), F32),
        pltpu.VMEM((HEADS, tile, HEAD_DIM), F32),
        pltpu.VMEM((HEADS, tile, HEAD_DIM), F32),
        pltpu.VMEM((subtiles, tile, HEAD_DIM), F32),
        pltpu.VMEM((subtiles, tile, HEAD_DIM), F32),
        pltpu.VMEM((subtiles, tile, d), BF16),
        pltpu.VMEM((subtiles, tile, d), BF16),
        pltpu.VMEM((subtiles, tile, d), BF16),
        pltpu.VMEM((subtiles, tile, d), BF16),
        pltpu.VMEM((subtiles, tile, d), BF16),
        pltpu.VMEM((subtiles, tile, d), F32),
        pltpu.VMEM((subtiles, tile, d), BF16),
        pltpu.VMEM((conv_rows, d), F32),
    ]
    if carry:
        scratch += [pltpu.VMEM((tile, HEAD_DIM), F32)] * 2
    return pl.pallas_call(
        functools.partial(_mixer_kernel, carry, tile, subtiles),
        out_shape=out_shape,
        grid=grid,
        in_specs=in_specs,
        out_specs=out_specs,
        scratch_shapes=scratch,
        compiler_params=pltpu.CompilerParams(
            dimension_semantics=("arbitrary",), vmem_limit_bytes=VMEM_LIMIT_BYTES),
        name="mixer_prompt" if carry else "mixer_sample",
    )(*inputs)


def _mlp_kernel(tile, subtiles, h_ref, n2_ref, w1_ref, w2_ref, nf_ref, y_ref):
    for sub in range(subtiles):
        rows = slice(sub * tile, (sub + 1) * tile)
        h = h_ref[rows, :]
        hn = _rms(h, n2_ref[...]).astype(BF16)
        a = jnp.maximum(_dot(hn, w1_ref[...]), 0.0)
        o = h + _dot((a * a).astype(BF16), w2_ref[...])
        y_ref[rows, :] = _rms(o, nf_ref[...])


def _mlp(h, n2, w1, w2, nf, name):
    n, d = h.shape
    tile, subtiles = MLP_TILE, MLP_SUBTILES
    tok = pl.BlockSpec((tile * subtiles, d), lambda i: (i, 0))
    weights = [n2, w1, w2, nf]
    return pl.pallas_call(
        functools.partial(_mlp_kernel, tile, subtiles),
        out_shape=jax.ShapeDtypeStruct((n, d), F32),
        grid=(n // (tile * subtiles),),
        in_specs=[tok] + [_resident(w.shape) for w in weights],
        out_specs=tok,
        compiler_params=pltpu.CompilerParams(
            dimension_semantics=("arbitrary",), vmem_limit_bytes=VMEM_LIMIT_BYTES),
        name=name,
    )(h, *weights)


def kernel(x_prompt, x_sample, state_ret, state_conv, norm1, w_in, ret_gn_w, conv_w, w_ret_out, w_conv_out, w_o, norm2, w_ff1, w_ff2, norm_f):
    depth = w_in.shape[0]
    assert depth == 1, "single-layer kernel"
    bp, lp, d = x_prompt.shape
    bs, ls, _ = x_sample.shape
    assert bp == 1 and ls == CHUNK
    assert lp % (MIXER_TILE * PROMPT_SUBTILES) == 0 and (bs * ls) % (MIXER_TILE * SAMPLE_SUBTILES) == 0
    assert lp % (MLP_TILE * MLP_SUBTILES) == 0 and (bs * ls) % (MLP_TILE * MLP_SUBTILES) == 0

    row = lambda v: v.reshape(1, -1)
    mixer_w = (row(norm1[0]), w_in[0].astype(BF16), row(ret_gn_w[0]), conv_w[0],
               w_ret_out[0].astype(BF16), w_conv_out[0].astype(BF16), w_o[0].astype(BF16))

    hp, ret_p, conv_p, ff1_bf, ff2_bf = _mixer(
        x_prompt.reshape(lp, d), None, None, *mixer_w, ff=(w_ff1[0], w_ff2[0]), carry=True)
    mlp_w = (row(norm2[0]), ff1_bf, ff2_bf, row(norm_f))
    conv_hist = jnp.pad(state_conv[0], ((0, 0), (HIST_ROWS - (CONV_W - 1), 0), (0, 0)))
    hs, ret_s, conv_s = _mixer(x_sample.reshape(bs * ls, d), state_ret[0], conv_hist, *mixer_w, carry=False)

    yp = _mlp(hp, *mlp_w, name="mlp_prompt").reshape(bp, lp, d)
    ys = _mlp(hs, *mlp_w, name="mlp_sample").reshape(bs, ls, d)
    keep = slice(HIST_ROWS - (CONV_W - 1), HIST_ROWS)
    return (yp, ys, ret_p[None, None], conv_p[None, None, keep],
            ret_s[None], conv_s[None, :, keep])
```

```python
import functools
import math

import jax
import jax.numpy as jnp
from jax import lax
from jax.experimental import pallas as pl
from jax.experimental.pallas import tpu as pltpu

CHUNK = 64
CHUNK_SHIFT = 6
HEADS = 8
HEAD_DIM = 128
CONV_W = 3
ROPE_BASE = 10000.0
EPS = 1e-6
PAST_LEN = 4096
HIST_ROWS = 8
BF16_SUBLANES = 16

MIXER_TILE = 256
PROMPT_SUBTILES = 2
SAMPLE_SUBTILES = 1
MLP_TILE = 512
MLP_SUBTILES = 2
CONVERT_STEPS = 32
VMEM_LIMIT_BYTES = 56 * 1024 * 1024

F32 = jnp.float32
BF16 = jnp.bfloat16


def _log_gamma(h):
    return math.log(1.0 - 2.0 ** (-5.0 - h))


def _rms(x, w):
    ms = jnp.mean(x * x, axis=-1, keepdims=True)
    return (x * lax.rsqrt(ms + EPS)) * w


def _dot(a, b):
    return jnp.dot(a, b, preferred_element_type=F32)


def _dot_nt(a, b):
    return lax.dot_general(a, b, (((1,), (1,)), ((), ())), preferred_element_type=F32)


def _dot_tn(a, b):
    return lax.dot_general(a, b, (((0,), (0,)), ((), ())), preferred_element_type=F32)


def _rope_cos_sin(pos):
    lane = lax.broadcasted_iota(jnp.int32, pos.shape, 1)
    half = HEAD_DIM // 2
    inv = jnp.exp((-math.log(ROPE_BASE)) * (lane & (half - 1)).astype(F32) / half)
    ang = pos * inv
    return jnp.cos(ang), jnp.where(lane < half, -1.0, 1.0) * jnp.sin(ang)


def _mixer_kernel(carry, tile, subtiles, *refs):
    n_chunks = tile // CHUNK
    if carry:
        (x_ref, n1_ref, gnw_ref, cw_ref, winf_ref, wrof_ref, wcof_ref, wof_ref, ff1_ref, ff2_ref,
         h_ref, sret_ref, sconv_ref, winb_ref, wrob_ref, wcob_ref, wob_ref, ff1_bf_ref, ff2_bf_ref,
         dmat, rowdec, coldec, cos_s, sin_s, xn_s, q_s, k_s, kd_s, v_s, sg_s, gated_s, ubuf,
         cosb_s, sinb_s, win_ref, wro_ref, wco_ref, wo_ref) = refs
        sin_ret_ref = sin_conv_ref = None
    else:
        (x_ref, sin_ret_ref, sin_conv_ref, n1_ref, gnw_ref, cw_ref, win_ref, wro_ref, wco_ref, wo_ref,
         h_ref, sret_ref, sconv_ref,
         dmat, rowdec, coldec, cos_s, sin_s, xn_s, q_s, k_s, kd_s, v_s, sg_s, gated_s, ubuf) = refs
    d_model = x_ref.shape[-1]
    step = pl.program_id(0)
    state_len = tile if carry else CHUNK

    @pl.when(step == 0)
    def _init():
        ii = lax.broadcasted_iota(jnp.int32, (tile, tile), 0)
        jj = lax.broadcasted_iota(jnp.int32, (tile, tile), 1)
        dist = jnp.abs(ii - jj).astype(F32)
        ci, cj = ii >> CHUNK_SHIFT, jj >> CHUNK_SHIFT
        keep = (cj <= ci) if carry else (cj == ci)
        row = lax.broadcasted_iota(jnp.int32, (tile, HEAD_DIM), 0)
        loc = (row if carry else row & (CHUNK - 1)).astype(F32)
        for h in range(HEADS):
            lg = _log_gamma(h)
            dmat[h] = jnp.where(keep, jnp.exp(lg * dist), 0.0)
            rowdec[h] = jnp.exp(lg * (loc + 1.0))
            coldec[h] = jnp.exp(lg * (state_len - 1.0 - loc))
        if carry:
            cosb_s[...], sinb_s[...] = _rope_cos_sin(row.astype(F32))
            sret_ref[...] = jnp.zeros_like(sret_ref)
            ubuf[0:HIST_ROWS, :] = jnp.zeros((HIST_ROWS, d_model), F32)
        else:
            cos, sin = _rope_cos_sin((PAST_LEN + (row & (CHUNK - 1))).astype(F32))
            for sub in range(subtiles):
                cos_s[sub], sin_s[sub] = cos, sin

    def tiles(t):
        w0, w1, w2 = cw_ref[0:1, :], cw_ref[1:2, :], cw_ref[2:3, :]

        for sub in range(subtiles):
            rows = slice(sub * tile, (sub + 1) * tile)
            x = x_ref[rows, :]
            xn_s[sub] = _rms(x, n1_ref[...]).astype(BF16)

            def proj(g):
                return _dot(xn_s[sub], win_ref[:, g * d_model:(g + 1) * d_model])

            if carry:
                first = jnp.zeros((8, HEAD_DIM), jnp.int32) + (t * subtiles + sub) * tile
                ct, st = _rope_cos_sin(first.astype(F32))
                ct, st = ct[0:1, :], st[0:1, :]
                cb, sb = cosb_s[...], sinb_s[...]
                cos_s[sub] = cb * ct - sb * st
                sin_s[sub] = sb * ct + cb * st

            def rope(z):
                return z * cos_s[sub] + pltpu.roll(z, HEAD_DIM // 2, 1) * sin_s[sub]

            q = proj(0)
            for h in range(HEADS):
                hs = slice(h * HEAD_DIM, (h + 1) * HEAD_DIM)
                q_s[sub, :, hs] = rope(q[:, hs]).astype(BF16)
            k = proj(1)
            for h in range(HEADS):
                hs = slice(h * HEAD_DIM, (h + 1) * HEAD_DIM)
                kr = rope(k[:, hs]) * (HEAD_DIM ** -0.5)
                k_s[sub, :, hs] = kr.astype(BF16)
                kd_s[sub, :, hs] = (kr * coldec[h]).astype(BF16)
            v_s[sub] = proj(2).astype(BF16)
            sg_s[sub] = jax.nn.silu(proj(3))

            for h in range(HEADS):
                hs = slice(h * HEAD_DIM, (h + 1) * HEAD_DIM)
                decay = math.exp(_log_gamma(h) * state_len)
                qh, kh, kdh, vh = q_s[sub, :, hs], k_s[sub, :, hs], kd_s[sub, :, hs], v_s[sub, :, hs]
                p = (_dot_nt(qh, kh) * dmat[h]).astype(BF16)
                o = _dot(p, vh)
                if carry:
                    s0 = sret_ref[h]
                    o = o + _dot(qh, s0.astype(BF16)) * rowdec[h]
                    sret_ref[h] = s0 * decay + _dot_tn(kdh, vh)
                else:
                    inter = []
                    for c in range(n_chunks):
                        rs = slice(c * CHUNK, (c + 1) * CHUNK)
                        seq = sub * n_chunks + c
                        s0 = sin_ret_ref[seq, h]
                        inter.append(_dot(qh[rs], s0.astype(BF16)))
                        sret_ref[seq, h] = s0 * decay + _dot_tn(kdh[rs], vh[rs])
                    o = o + jnp.concatenate(inter, axis=0) * rowdec[h]
                mu = jnp.mean(o, axis=-1, keepdims=True)
                d = o - mu
                var = jnp.mean(d * d, axis=-1, keepdims=True)
                on = (d * lax.rsqrt(var + EPS)) * gnw_ref[:, hs]
                gated_s[sub, :, hs] = (on * sg_s[sub, :, hs]).astype(BF16)
            ret_y = _dot(gated_s[sub], wro_ref[...])
            mix = jax.nn.sigmoid(proj(7)) * ret_y

            u = proj(5) * proj(6)
            if carry:
                base = HIST_ROWS + sub * tile
                ubuf[base:base + tile, :] = u
                y = (w0 * ubuf[base - 2:base - 2 + tile, :]
                     + w1 * ubuf[base - 1:base - 1 + tile, :]) + w2 * u
            else:
                ys = []
                for c in range(n_chunks):
                    seq = sub * n_chunks + c
                    base = seq * (CHUNK + HIST_ROWS) + HIST_ROWS
                    ubuf[base - HIST_ROWS:base, :] = sin_conv_ref[seq]
                    ubuf[base:base + CHUNK, :] = u[c * CHUNK:(c + 1) * CHUNK]
                    ys.append((w0 * ubuf[base - 2:base - 2 + CHUNK, :]
                               + w1 * ubuf[base - 1:base - 1 + CHUNK, :])
                              + w2 * ubuf[base:base + CHUNK, :])
                    sconv_ref[seq] = ubuf[base + CHUNK - HIST_ROWS:base + CHUNK, :]
                y = jnp.concatenate(ys, axis=0)
            conv_y = _dot((proj(4) * y).astype(BF16), wco_ref[...])
            mix = mix + jax.nn.sigmoid(proj(8)) * conv_y

            h_ref[rows, :] = x + _dot(mix.astype(BF16), wo_ref[...])

        if carry:
            tail = ubuf[subtiles * tile:subtiles * tile + HIST_ROWS, :]
            sconv_ref[...] = tail
            ubuf[0:HIST_ROWS, :] = tail

    if carry:
        n_conv = win_ref.shape[0] // winf_ref.shape[0]

        @pl.when(step < n_conv)
        def _convert():
            for f_ref, b_ref, w_ref in ((winf_ref, winb_ref, win_ref), (wrof_ref, wrob_ref, wro_ref),
                                        (wcof_ref, wcob_ref, wco_ref), (wof_ref, wob_ref, wo_ref)):
                slab = f_ref.shape[0]
                v = f_ref[...].astype(BF16)
                b_ref[...] = v
                w_ref[pl.ds(pl.multiple_of(step * slab, slab), slab), :] = v

        @pl.when(step >= n_conv)
        def _tiles():
            ff1_bf_ref[...] = ff1_ref[...].astype(BF16)
            ff2_bf_ref[...] = ff2_ref[...].astype(BF16)
            tiles(step - n_conv)
    else:
        tiles(step)


def _resident(shape):
    nd = len(shape)
    return pl.BlockSpec(shape, lambda i: (0,) * nd, pipeline_mode=pl.Buffered(1))


def _mixer(x, state_ret, state_conv, n1, gnw, cw, weights, ff=None, *, carry):
    n, d = x.shape
    tile = MIXER_TILE
    subtiles = PROMPT_SUBTILES if carry else SAMPLE_SUBTILES
    rows = tile * subtiles
    n_seqs = rows // CHUNK
    tile_steps = n // rows
    n_conv = CONVERT_STEPS if carry else 0
    grid = (n_conv + tile_steps,)
    tile_idx = lambda i: jnp.maximum(i - n_conv, 0)
    tok = pl.BlockSpec((rows, d), lambda i: (tile_idx(i), 0))
    small = [n1, gnw, cw]
    small_specs = [_resident(w.shape) for w in small]

    def slabs(w, steps, index):
        slab = w.shape[0] // steps
        assert slab * steps == w.shape[0] and slab % BF16_SUBLANES == 0
        return pl.BlockSpec((slab, w.shape[1]), lambda i: (index(i), 0))

    if carry:
        conv_idx = lambda i: jnp.minimum(i, n_conv - 1)
        w_slabs = [slabs(w, n_conv, conv_idx) for w in weights]
        ff_slabs = [slabs(w, tile_steps, tile_idx) for w in ff]
        inputs = [x] + small + list(weights) + list(ff)
        in_specs = [tok] + small_specs + w_slabs + ff_slabs
        out_shape = (jax.ShapeDtypeStruct((n, d), F32),
                     jax.ShapeDtypeStruct((HEADS, HEAD_DIM, HEAD_DIM), F32),
                     jax.ShapeDtypeStruct((HIST_ROWS, d), F32),
                     *[jax.ShapeDtypeStruct(w.shape, BF16) for w in (*weights, *ff)])
        out_specs = (tok,
                     pl.BlockSpec((HEADS, HEAD_DIM, HEAD_DIM), lambda i: (0, 0, 0)),
                     pl.BlockSpec((HIST_ROWS, d), lambda i: (0, 0)),
                     *w_slabs, *ff_slabs)
        conv_rows = rows + HIST_ROWS
    else:
        ret_spec = pl.BlockSpec((n_seqs, HEADS, HEAD_DIM, HEAD_DIM), lambda i: (i, 0, 0, 0))
        conv_spec = pl.BlockSpec((n_seqs, HIST_ROWS, d), lambda i: (i, 0, 0))
        inputs = [x, state_ret, state_conv] + small + list(weights)
        in_specs = [tok, ret_spec, conv_spec] + small_specs + [_resident(w.shape) for w in weights]
        out_shape = (jax.ShapeDtypeStruct((n, d), F32),
                     jax.ShapeDtypeStruct(state_ret.shape, F32),
                     jax.ShapeDtypeStruct(state_conv.shape, F32))
        out_specs = (tok, ret_spec, conv_spec)
        conv_rows = n_seqs * (CHUNK + HIST_ROWS)
    scratch = [
        pltpu.VMEM((HEADS, tile, tile), F32),
        pltpu.VMEM((HEADS, tile, HEAD_DIM), F32),
        pltpu.VMEM((HEADS, tile, HEAD_DIM), F32),
        pltpu.VMEM((subtiles, tile, HEAD_DIM), F32),
        pltpu.VMEM((subtiles, tile, HEAD_DIM), F32),
        pltpu.VMEM((subtiles, tile, d), BF16),
        pltpu.VMEM((subtiles, tile, d), BF16),
        pltpu.VMEM((subtiles, tile, d), BF16),
        pltpu.VMEM((subtiles, tile, d), BF16),
        pltpu.VMEM((subtiles, tile, d), BF16),
        pltpu.VMEM((subtiles, tile, d), F32),
        pltpu.VMEM((subtiles, tile, d), BF16),
        pltpu.VMEM((conv_rows, d), F32),
    ]
    if carry:
        scratch += [pltpu.VMEM((tile, HEAD_DIM), F32)] * 2
        scratch += [pltpu.VMEM(w.shape, BF16) for w in weights]
    return pl.pallas_call(
        functools.partial(_mixer_kernel, carry, tile, subtiles),
        out_shape=out_shape,
        grid=grid,
        in_specs=in_specs,
        out_specs=out_specs,
        scratch_shapes=scratch,
        compiler_params=pltpu.CompilerParams(
            dimension_semantics=("arbitrary",), vmem_limit_bytes=VMEM_LIMIT_BYTES),
        name="mixer_prompt" if carry else "mixer_sample",
    )(*inputs)


def _mlp_kernel(tile, subtiles, h_ref, n2_ref, w1_ref, w2_ref, nf_ref, y_ref):
    for sub in range(subtiles):
        rows = slice(sub * tile, (sub + 1) * tile)
        h = h_ref[rows, :]
        hn = _rms(h, n2_ref[...]).astype(BF16)
        a = jnp.maximum(_dot(hn, w1_ref[...]), 0.0)
        o = h + _dot((a * a).astype(BF16), w2_ref[...])
        y_ref[rows, :] = _rms(o, nf_ref[...])


def _mlp(h, n2, w1, w2, nf, name):
    n, d = h.shape
    tile, subtiles = MLP_TILE, MLP_SUBTILES
    tok = pl.BlockSpec((tile * subtiles, d), lambda i: (i, 0))
    weights = [n2, w1, w2, nf]
    return pl.pallas_call(
        functools.partial(_mlp_kernel, tile, subtiles),
        out_shape=jax.ShapeDtypeStruct((n, d), F32),
        grid=(n // (tile * subtiles),),
        in_specs=[tok] + [_resident(w.shape) for w in weights],
        out_specs=tok,
        compiler_params=pltpu.CompilerParams(
            dimension_semantics=("arbitrary",), vmem_limit_bytes=VMEM_LIMIT_BYTES),
        name=name,
    )(h, *weights)


def kernel(x_prompt, x_sample, state_ret, state_conv, norm1, w_in, ret_gn_w, conv_w, w_ret_out, w_conv_out, w_o, norm2, w_ff1, w_ff2, norm_f):
    depth = w_in.shape[0]
    assert depth == 1, "single-layer kernel"
    bp, lp, d = x_prompt.shape
    bs, ls, _ = x_sample.shape
    assert bp == 1 and ls == CHUNK
    assert lp % (MIXER_TILE * PROMPT_SUBTILES) == 0 and (bs * ls) % (MIXER_TILE * SAMPLE_SUBTILES) == 0
    assert lp % (MLP_TILE * MLP_SUBTILES) == 0 and (bs * ls) % (MLP_TILE * MLP_SUBTILES) == 0

    row = lambda v: v.reshape(1, -1)
    small = (row(norm1[0]), row(ret_gn_w[0]), conv_w[0])

    hp, ret_p, conv_p, *bf = _mixer(
        x_prompt.reshape(lp, d), None, None, *small,
        (w_in[0], w_ret_out[0], w_conv_out[0], w_o[0]), ff=(w_ff1[0], w_ff2[0]), carry=True)
    mixer_bf, (ff1_bf, ff2_bf) = bf[:4], bf[4:]
    mlp_w = (row(norm2[0]), ff1_bf, ff2_bf, row(norm_f))
    conv_hist = jnp.pad(state_conv[0], ((0, 0), (HIST_ROWS - (CONV_W - 1), 0), (0, 0)))
    hs, ret_s, conv_s = _mixer(x_sample.reshape(bs * ls, d), state_ret[0], conv_hist, *small, mixer_bf, carry=False)

    yp = _mlp(hp, *mlp_w, name="mlp_prompt").reshape(bp, lp, d)
    ys = _mlp(hs, *mlp_w, name="mlp_sample").reshape(bs, ls, d)
    keep = slice(HIST_ROWS - (CONV_W - 1), HIST_ROWS)
    return (yp, ys, ret_p[None, None], conv_p[None, None, keep],
            ret_s[None], conv_s[None, :, keep])
```

```python
import functools
import math

import jax
import jax.numpy as jnp
from jax import lax
from jax.experimental import pallas as pl
from jax.experimental.pallas import tpu as pltpu

CHUNK = 64
CHUNK_SHIFT = 6
HEADS = 8
HEAD_DIM = 128
CONV_W = 3
ROPE_BASE = 10000.0
EPS = 1e-6
PAST_LEN = 4096
HIST_ROWS = 8
BF16_SUBLANES = 16

MIXER_TILE = 256
PROMPT_SUBTILES = 2
SAMPLE_SUBTILES = 1
MLP_TILE = 512
MLP_SUBTILES = 2
CONVERT_STEPS = 16
VMEM_LIMIT_BYTES = 60 * 1024 * 1024

F32 = jnp.float32
BF16 = jnp.bfloat16


def _log_gamma(h):
    return math.log(1.0 - 2.0 ** (-5.0 - h))


def _rms(x, w):
    ms = jnp.mean(x * x, axis=-1, keepdims=True)
    return (x * lax.rsqrt(ms + EPS)) * w


def _dot(a, b):
    return jnp.dot(a, b, preferred_element_type=F32)


def _dot_nt(a, b):
    return lax.dot_general(a, b, (((1,), (1,)), ((), ())), preferred_element_type=F32)


def _dot_tn(a, b):
    return lax.dot_general(a, b, (((0,), (0,)), ((), ())), preferred_element_type=F32)


def _rope_cos_sin(pos):
    lane = lax.broadcasted_iota(jnp.int32, pos.shape, 1)
    half = HEAD_DIM // 2
    inv = jnp.exp((-math.log(ROPE_BASE)) * (lane & (half - 1)).astype(F32) / half)
    ang = pos * inv
    return jnp.cos(ang), jnp.where(lane < half, -1.0, 1.0) * jnp.sin(ang)


def _mixer_kernel(carry, tile, subtiles, *refs):
    n_chunks = tile // CHUNK
    if carry:
        (x_ref, n1_ref, gnw_ref, cw_ref, winf_ref, wrof_ref, wcof_ref, wof_ref, ff1_ref, ff2_ref,
         h_ref, sret_ref, sconv_ref, winb_ref, wrob_ref, wcob_ref, wob_ref, ff1_bf_ref, ff2_bf_ref,
         dmat, rowdec, coldec, cos_s, sin_s, xn_s, q_s, k_s, kd_s, v_s, sg_s, gated_s, ubuf,
         cosb_s, sinb_s, win_ref, wro_ref, wco_ref, wo_ref) = refs
        sin_ret_ref = sin_conv_ref = None
    else:
        (x_ref, sin_ret_ref, sin_conv_ref, n1_ref, gnw_ref, cw_ref, win_ref, wro_ref, wco_ref, wo_ref,
         h_ref, sret_ref, sconv_ref,
         dmat, rowdec, coldec, cos_s, sin_s, xn_s, q_s, k_s, kd_s, v_s, sg_s, gated_s, ubuf) = refs
    d_model = x_ref.shape[-1]
    step = pl.program_id(0)
    state_len = tile if carry else CHUNK

    @pl.when(step == 0)
    def _init():
        ii = lax.broadcasted_iota(jnp.int32, (tile, tile), 0)
        jj = lax.broadcasted_iota(jnp.int32, (tile, tile), 1)
        dist = jnp.abs(ii - jj).astype(F32)
        ci, cj = ii >> CHUNK_SHIFT, jj >> CHUNK_SHIFT
        keep = (cj <= ci) if carry else (cj == ci)
        row = lax.broadcasted_iota(jnp.int32, (tile, HEAD_DIM), 0)
        loc = (row if carry else row & (CHUNK - 1)).astype(F32)
        for h in range(HEADS):
            lg = _log_gamma(h)
            dmat[h] = jnp.where(keep, jnp.exp(lg * dist), 0.0)
            rowdec[h] = jnp.exp(lg * (loc + 1.0))
            coldec[h] = jnp.exp(lg * (state_len - 1.0 - loc))
        if carry:
            cosb_s[...], sinb_s[...] = _rope_cos_sin(row.astype(F32))
            sret_ref[...] = jnp.zeros_like(sret_ref)
            ubuf[0:HIST_ROWS, :] = jnp.zeros((HIST_ROWS, d_model), F32)
        else:
            cos, sin = _rope_cos_sin((PAST_LEN + (row & (CHUNK - 1))).astype(F32))
            for sub in range(subtiles):
                cos_s[sub], sin_s[sub] = cos, sin

    def tiles(t):
        w0, w1, w2 = cw_ref[0:1, :], cw_ref[1:2, :], cw_ref[2:3, :]

        for sub in range(subtiles):
            rows = slice(sub * tile, (sub + 1) * tile)
            x = x_ref[rows, :]
            xn_s[sub] = _rms(x, n1_ref[...]).astype(BF16)

            def proj(g):
                return _dot(xn_s[sub], win_ref[:, g * d_model:(g + 1) * d_model])

            if carry:
                first = jnp.zeros((8, HEAD_DIM), jnp.int32) + (t * subtiles + sub) * tile
                ct, st = _rope_cos_sin(first.astype(F32))
                ct, st = ct[0:1, :], st[0:1, :]
                cb, sb = cosb_s[...], sinb_s[...]
                cos_s[sub] = cb * ct - sb * st
                sin_s[sub] = sb * ct + cb * st

            def rope(z):
                return z * cos_s[sub] + pltpu.roll(z, HEAD_DIM // 2, 1) * sin_s[sub]

            q = proj(0)
            for h in range(HEADS):
                hs = slice(h * HEAD_DIM, (h + 1) * HEAD_DIM)
                q_s[sub, :, hs] = rope(q[:, hs]).astype(BF16)
            k = proj(1)
            for h in range(HEADS):
                hs = slice(h * HEAD_DIM, (h + 1) * HEAD_DIM)
                kr = rope(k[:, hs]) * (HEAD_DIM ** -0.5)
                k_s[sub, :, hs] = kr.astype(BF16)
                kd_s[sub, :, hs] = (kr * coldec[h]).astype(BF16)
            v_s[sub] = proj(2).astype(BF16)
            sg_s[sub] = jax.nn.silu(proj(3))

            for h in range(HEADS):
                hs = slice(h * HEAD_DIM, (h + 1) * HEAD_DIM)
                decay = math.exp(_log_gamma(h) * state_len)
                qh, kh, kdh, vh = q_s[sub, :, hs], k_s[sub, :, hs], kd_s[sub, :, hs], v_s[sub, :, hs]
                p = (_dot_nt(qh, kh) * dmat[h]).astype(BF16)
                o = _dot(p, vh)
                if carry:
                    s0 = sret_ref[h]
                    o = o + _dot(qh, s0.astype(BF16)) * rowdec[h]
                    sret_ref[h] = s0 * decay + _dot_tn(kdh, vh)
                else:
                    inter = []
                    for c in range(n_chunks):
                        rs = slice(c * CHUNK, (c + 1) * CHUNK)
                        seq = sub * n_chunks + c
                        s0 = sin_ret_ref[seq, h]
                        inter.append(_dot(qh[rs], s0.astype(BF16)))
                        sret_ref[seq, h] = s0 * decay + _dot_tn(kdh[rs], vh[rs])
                    o = o + jnp.concatenate(inter, axis=0) * rowdec[h]
                mu = jnp.mean(o, axis=-1, keepdims=True)
                d = o - mu
                var = jnp.mean(d * d, axis=-1, keepdims=True)
                on = (d * lax.rsqrt(var + EPS)) * gnw_ref[:, hs]
                gated_s[sub, :, hs] = (on * sg_s[sub, :, hs]).astype(BF16)
            ret_y = _dot(gated_s[sub], wro_ref[...])
            mix = jax.nn.sigmoid(proj(7)) * ret_y

            u = proj(5) * proj(6)
            if carry:
                base = HIST_ROWS + sub * tile
                ubuf[base:base + tile, :] = u
                y = (w0 * ubuf[base - 2:base - 2 + tile, :]
                     + w1 * ubuf[base - 1:base - 1 + tile, :]) + w2 * u
            else:
                ys = []
                for c in range(n_chunks):
                    seq = sub * n_chunks + c
                    base = seq * (CHUNK + HIST_ROWS) + HIST_ROWS
                    ubuf[base - HIST_ROWS:base, :] = sin_conv_ref[seq]
                    ubuf[base:base + CHUNK, :] = u[c * CHUNK:(c + 1) * CHUNK]
                    ys.append((w0 * ubuf[base - 2:base - 2 + CHUNK, :]
                               + w1 * ubuf[base - 1:base - 1 + CHUNK, :])
                              + w2 * ubuf[base:base + CHUNK, :])
                    sconv_ref[seq] = ubuf[base + CHUNK - HIST_ROWS:base + CHUNK, :]
                y = jnp.concatenate(ys, axis=0)
            conv_y = _dot((proj(4) * y).astype(BF16), wco_ref[...])
            mix = mix + jax.nn.sigmoid(proj(8)) * conv_y

            h_ref[rows, :] = x + _dot(mix.astype(BF16), wo_ref[...])

        if carry:
            tail = ubuf[subtiles * tile:subtiles * tile + HIST_ROWS, :]
            sconv_ref[...] = tail
            ubuf[0:HIST_ROWS, :] = tail

    if carry:
        n_conv = win_ref.shape[0] // winf_ref.shape[0]

        @pl.when(step < n_conv)
        def _convert():
            for f_ref, b_ref, w_ref in ((winf_ref, winb_ref, win_ref), (wrof_ref, wrob_ref, wro_ref),
                                        (wcof_ref, wcob_ref, wco_ref), (wof_ref, wob_ref, wo_ref)):
                slab = f_ref.shape[0]
                v = f_ref[...].astype(BF16)
                b_ref[...] = v
                w_ref[pl.ds(pl.multiple_of(step * slab, slab), slab), :] = v

        @pl.when(step >= n_conv)
        def _tiles():
            ff1_bf_ref[...] = ff1_ref[...].astype(BF16)
            ff2_bf_ref[...] = ff2_ref[...].astype(BF16)
            tiles(step - n_conv)
    else:
        tiles(step)


def _resident(shape):
    nd = len(shape)
    return pl.BlockSpec(shape, lambda i: (0,) * nd, pipeline_mode=pl.Buffered(1))


def _mixer(x, state_ret, state_conv, n1, gnw, cw, weights, ff=None, *, carry):
    n, d = x.shape
    tile = MIXER_TILE
    subtiles = PROMPT_SUBTILES if carry else SAMPLE_SUBTILES
    rows = tile * subtiles
    n_seqs = rows // CHUNK
    tile_steps = n // rows
    n_conv = CONVERT_STEPS if carry else 0
    grid = (n_conv + tile_steps,)
    tile_idx = lambda i: jnp.maximum(i - n_conv, 0)
    tok = pl.BlockSpec((rows, d), lambda i: (tile_idx(i), 0))
    small = [n1, gnw, cw]
    small_specs = [_resident(w.shape) for w in small]

    def slabs(w, steps, index):
        slab = w.shape[0] // steps
        assert slab * steps == w.shape[0] and slab % BF16_SUBLANES == 0
        return pl.BlockSpec((slab, w.shape[1]), lambda i: (index(i), 0))

    if carry:
        conv_idx = lambda i: jnp.minimum(i, n_conv - 1)
        w_slabs = [slabs(w, n_conv, conv_idx) for w in weights]
        ff_slabs = [slabs(w, tile_steps, tile_idx) for w in ff]
        inputs = [x] + small + list(weights) + list(ff)
        in_specs = [tok] + small_specs + w_slabs + ff_slabs
        out_shape = (jax.ShapeDtypeStruct((n, d), F32),
                     jax.ShapeDtypeStruct((HEADS, HEAD_DIM, HEAD_DIM), F32),
                     jax.ShapeDtypeStruct((HIST_ROWS, d), F32),
                     *[jax.ShapeDtypeStruct(w.shape, BF16) for w in (*weights, *ff)])
        out_specs = (tok,
                     pl.BlockSpec((HEADS, HEAD_DIM, HEAD_DIM), lambda i: (0, 0, 0)),
                     pl.BlockSpec((HIST_ROWS, d), lambda i: (0, 0)),
                     *w_slabs, *ff_slabs)
        conv_rows = rows + HIST_ROWS
    else:
        ret_spec = pl.BlockSpec((n_seqs, HEADS, HEAD_DIM, HEAD_DIM), lambda i: (i, 0, 0, 0))
        conv_spec = pl.BlockSpec((n_seqs, HIST_ROWS, d), lambda i: (i, 0, 0))
        inputs = [x, state_ret, state_conv] + small + list(weights)
        in_specs = [tok, ret_spec, conv_spec] + small_specs + [_resident(w.shape) for w in weights]
        out_shape = (jax.ShapeDtypeStruct((n, d), F32),
                     jax.ShapeDtypeStruct(state_ret.shape, F32),
                     jax.ShapeDtypeStruct(state_conv.shape, F32))
        out_specs = (tok, ret_spec, conv_spec)
        conv_rows = n_seqs * (CHUNK + HIST_ROWS)
    scratch = [
        pltpu.VMEM((HEADS, tile, tile), F32),
        pltpu.VMEM((HEADS, tile, HEAD_DIM), F32),
        pltpu.VMEM((HEADS, tile, HEAD_DIM), F32),
        pltpu.VMEM((subtiles, tile, HEAD_DIM), F32),
        pltpu.VMEM((subtiles, tile, HEAD_DIM), F32),
        pltpu.VMEM((subtiles, tile, d), BF16),
        pltpu.VMEM((subtiles, tile, d), BF16),
        pltpu.VMEM((subtiles, tile, d), BF16),
        pltpu.VMEM((subtiles, tile, d), BF16),
        pltpu.VMEM((subtiles, tile, d), BF16),
        pltpu.VMEM((subtiles, tile, d), F32),
        pltpu.VMEM((subtiles, tile, d), BF16),
        pltpu.VMEM((conv_rows, d), F32),
    ]
    if carry:
        scratch += [pltpu.VMEM((tile, HEAD_DIM), F32)] * 2
        scratch += [pltpu.VMEM(w.shape, BF16) for w in weights]
    return pl.pallas_call(
        functools.partial(_mixer_kernel, carry, tile, subtiles),
        out_shape=out_shape,
        grid=grid,
        in_specs=in_specs,
        out_specs=out_specs,
        scratch_shapes=scratch,
        compiler_params=pltpu.CompilerParams(
            dimension_semantics=("arbitrary",), vmem_limit_bytes=VMEM_LIMIT_BYTES),
        name="mixer_prompt" if carry else "mixer_sample",
    )(*inputs)


def _mlp_kernel(tile, subtiles, h_ref, n2_ref, w1_ref, w2_ref, nf_ref, y_ref):
    for sub in range(subtiles):
        rows = slice(sub * tile, (sub + 1) * tile)
        h = h_ref[rows, :]
        hn = _rms(h, n2_ref[...]).astype(BF16)
        a = jnp.maximum(_dot(hn, w1_ref[...]), 0.0)
        o = h + _dot((a * a).astype(BF16), w2_ref[...])
        y_ref[rows, :] = _rms(o, nf_ref[...])


def _mlp(h, n2, w1, w2, nf, name):
    n, d = h.shape
    tile, subtiles = MLP_TILE, MLP_SUBTILES
    tok = pl.BlockSpec((tile * subtiles, d), lambda i: (i, 0))
    weights = [n2, w1, w2, nf]
    return pl.pallas_call(
        functools.partial(_mlp_kernel, tile, subtiles),
        out_shape=jax.ShapeDtypeStruct((n, d), F32),
        grid=(n // (tile * subtiles),),
        in_specs=[tok] + [_resident(w.shape) for w in weights],
        out_specs=tok,
        compiler_params=pltpu.CompilerParams(
            dimension_semantics=("arbitrary",), vmem_limit_bytes=VMEM_LIMIT_BYTES),
        name=name,
    )(h, *weights)


def kernel(x_prompt, x_sample, state_ret, state_conv, norm1, w_in, ret_gn_w, conv_w, w_ret_out, w_conv_out, w_o, norm2, w_ff1, w_ff2, norm_f):
    depth = w_in.shape[0]
    assert depth == 1, "single-layer kernel"
    bp, lp, d = x_prompt.shape
    bs, ls, _ = x_sample.shape
    assert bp == 1 and ls == CHUNK
    assert lp % (MIXER_TILE * PROMPT_SUBTILES) == 0 and (bs * ls) % (MIXER_TILE * SAMPLE_SUBTILES) == 0
    assert lp % (MLP_TILE * MLP_SUBTILES) == 0 and (bs * ls) % (MLP_TILE * MLP_SUBTILES) == 0

    row = lambda v: v.reshape(1, -1)
    small = (row(norm1[0]), row(ret_gn_w[0]), conv_w[0])

    hp, ret_p, conv_p, *bf = _mixer(
        x_prompt.reshape(lp, d), None, None, *small,
        (w_in[0], w_ret_out[0], w_conv_out[0], w_o[0]), ff=(w_ff1[0], w_ff2[0]), carry=True)
    mixer_bf, (ff1_bf, ff2_bf) = bf[:4], bf[4:]
    mlp_w = (row(norm2[0]), ff1_bf, ff2_bf, row(norm_f))
    conv_hist = jnp.pad(state_conv[0], ((0, 0), (HIST_ROWS - (CONV_W - 1), 0), (0, 0)))
    hs, ret_s, conv_s = _mixer(x_sample.reshape(bs * ls, d), state_ret[0], conv_hist, *small, mixer_bf, carry=False)

    yp = _mlp(hp, *mlp_w, name="mlp_prompt").reshape(bp, lp, d)
    ys = _mlp(hs, *mlp_w, name="mlp_sample").reshape(bs, ls, d)
    keep = slice(HIST_ROWS - (CONV_W - 1), HIST_ROWS)
    return (yp, ys, ret_p[None, None], conv_p[None, None, keep],
            ret_s[None], conv_s[None, :, keep])
```

```python
import functools
import math

import jax
import jax.numpy as jnp
from jax import lax
from jax.experimental import pallas as pl
from jax.experimental.pallas import tpu as pltpu

CHUNK = 64
CHUNK_SHIFT = 6
HEADS = 8
HEAD_DIM = 128
CONV_W = 3
ROPE_BASE = 10000.0
EPS = 1e-6
PAST_LEN = 4096
HIST_ROWS = 8
BF16_SUBLANES = 16

MIXER_TILE = 256
PROMPT_SUBTILES = 2
SAMPLE_SUBTILES = 1
MLP_TILE = 512
MLP_SUBTILES = 2
CONVERT_STEPS = 16
VMEM_LIMIT_BYTES = 60 * 1024 * 1024

F32 = jnp.float32
BF16 = jnp.bfloat16


def _log_gamma(h):
    return math.log(1.0 - 2.0 ** (-5.0 - h))


def _rms(x, w):
    ms = jnp.mean(x * x, axis=-1, keepdims=True)
    return (x * lax.rsqrt(ms + EPS)) * w


def _dot(a, b):
    return jnp.dot(a, b, preferred_element_type=F32)


def _dot_nt(a, b):
    return lax.dot_general(a, b, (((1,), (1,)), ((), ())), preferred_element_type=F32)


def _dot_tn(a, b):
    return lax.dot_general(a, b, (((0,), (0,)), ((), ())), preferred_element_type=F32)


def _rope_cos_sin(pos):
    lane = lax.broadcasted_iota(jnp.int32, pos.shape, 1)
    half = HEAD_DIM // 2
    inv = jnp.exp((-math.log(ROPE_BASE)) * (lane & (half - 1)).astype(F32) / half)
    ang = pos * inv
    return jnp.cos(ang), jnp.where(lane < half, -1.0, 1.0) * jnp.sin(ang)


def _mixer_kernel(carry, tile, subtiles, *refs):
    n_chunks = tile // CHUNK
    if carry:
        (x_ref, n1_ref, gnw_ref, cw_ref, winf_ref, wrof_ref, wcof_ref, wof_ref, ff1_ref, ff2_ref,
         h_ref, sret_ref, sconv_ref, winb_ref, wrob_ref, wcob_ref, wob_ref, ff1_bf_ref, ff2_bf_ref,
         dmat, rowdec, coldec, cos_s, sin_s, xn_s, q_s, k_s, kd_s, v_s, sg_s, gated_s, ubuf,
         cosb_s, sinb_s, win_ref, wro_ref, wco_ref, wo_ref) = refs
        sin_ret_ref = sin_conv_ref = None
    else:
        (x_ref, sin_ret_ref, sin_conv_ref, n1_ref, gnw_ref, cw_ref, win_ref, wro_ref, wco_ref, wo_ref,
         h_ref, sret_ref, sconv_ref,
         dmat, rowdec, coldec, cos_s, sin_s, xn_s, q_s, k_s, kd_s, v_s, sg_s, gated_s, ubuf) = refs
    d_model = x_ref.shape[-1]
    step = pl.program_id(0)
    state_len = tile if carry else CHUNK

    @pl.when(step == 0)
    def _init():
        ii = lax.broadcasted_iota(jnp.int32, (tile, tile), 0)
        jj = lax.broadcasted_iota(jnp.int32, (tile, tile), 1)
        dist = jnp.abs(ii - jj).astype(F32)
        ci, cj = ii >> CHUNK_SHIFT, jj >> CHUNK_SHIFT
        keep = (cj <= ci) if carry else (cj == ci)
        row = lax.broadcasted_iota(jnp.int32, (tile, HEAD_DIM), 0)
        loc = (row if carry else row & (CHUNK - 1)).astype(F32)
        for h in range(HEADS):
            lg = _log_gamma(h)
            dmat[h] = jnp.where(keep, jnp.exp(lg * dist), 0.0)
            rowdec[h] = jnp.exp(lg * (loc + 1.0))
            coldec[h] = jnp.exp(lg * (state_len - 1.0 - loc))
        if carry:
            cosb_s[...], sinb_s[...] = _rope_cos_sin(row.astype(F32))
            sret_ref[...] = jnp.zeros_like(sret_ref)
            ubuf[0:HIST_ROWS, :] = jnp.zeros((HIST_ROWS, d_model), F32)
        else:
            cos, sin = _rope_cos_sin((PAST_LEN + (row & (CHUNK - 1))).astype(F32))
            for sub in range(subtiles):
                cos_s[sub], sin_s[sub] = cos, sin

    def tiles(t):
        w0, w1, w2 = cw_ref[0:1, :], cw_ref[1:2, :], cw_ref[2:3, :]

        for sub in range(subtiles):
            rows = slice(sub * tile, (sub + 1) * tile)
            x = x_ref[rows, :]
            xn_s[sub] = _rms(x, n1_ref[...]).astype(BF16)

            def proj(g):
                return _dot(xn_s[sub], win_ref[:, g * d_model:(g + 1) * d_model])

            if carry:
                first = jnp.zeros((8, HEAD_DIM), jnp.int32) + (t * subtiles + sub) * tile
                ct, st = _rope_cos_sin(first.astype(F32))
                ct, st = ct[0:1, :], st[0:1, :]
                cb, sb = cosb_s[...], sinb_s[...]
                cos_s[sub] = cb * ct - sb * st
                sin_s[sub] = sb * ct + cb * st

            def rope(z):
                return z * cos_s[sub] + pltpu.roll(z, HEAD_DIM // 2, 1) * sin_s[sub]

            q = proj(0)
            for h in range(HEADS):
                hs = slice(h * HEAD_DIM, (h + 1) * HEAD_DIM)
                q_s[sub, :, hs] = rope(q[:, hs]).astype(BF16)
            k = proj(1)
            for h in range(HEADS):
                hs = slice(h * HEAD_DIM, (h + 1) * HEAD_DIM)
                kr = rope(k[:, hs]) * (HEAD_DIM ** -0.5)
                k_s[sub, :, hs] = kr.astype(BF16)
                kd_s[sub, :, hs] = (kr * coldec[h]).astype(BF16)
            v_s[sub] = proj(2).astype(BF16)

            def scores(h):
                hs = slice(h * HEAD_DIM, (h + 1) * HEAD_DIM)
                return (_dot_nt(q_s[sub, :, hs], k_s[sub, :, hs]) * dmat[h]).astype(BF16)

            p_next = scores(0)
            sg_s[sub] = jax.nn.silu(proj(3))
            for h in range(HEADS):
                hs = slice(h * HEAD_DIM, (h + 1) * HEAD_DIM)
                decay = math.exp(_log_gamma(h) * state_len)
                qh, kdh, vh = q_s[sub, :, hs], kd_s[sub, :, hs], v_s[sub, :, hs]
                p = p_next
                if h + 1 < HEADS:
                    p_next = scores(h + 1)
                o = _dot(p, vh)
                if carry:
                    s0 = sret_ref[h]
                    o = o + _dot(qh, s0.astype(BF16)) * rowdec[h]
                    sret_ref[h] = s0 * decay + _dot_tn(kdh, vh)
                else:
                    inter = []
                    for c in range(n_chunks):
                        rs = slice(c * CHUNK, (c + 1) * CHUNK)
                        seq = sub * n_chunks + c
                        s0 = sin_ret_ref[seq, h]
                        inter.append(_dot(qh[rs], s0.astype(BF16)))
                        sret_ref[seq, h] = s0 * decay + _dot_tn(kdh[rs], vh[rs])
                    o = o + jnp.concatenate(inter, axis=0) * rowdec[h]
                mu = jnp.mean(o, axis=-1, keepdims=True)
                d = o - mu
                var = jnp.mean(d * d, axis=-1, keepdims=True)
                on = (d * lax.rsqrt(var + EPS)) * gnw_ref[:, hs]
                gated_s[sub, :, hs] = (on * sg_s[sub, :, hs]).astype(BF16)

            u = proj(5) * proj(6)
            ret_y = _dot(gated_s[sub], wro_ref[...])
            mix = jax.nn.sigmoid(proj(7)) * ret_y
            if carry:
                base = HIST_ROWS + sub * tile
                ubuf[base:base + tile, :] = u
                y = (w0 * ubuf[base - 2:base - 2 + tile, :]
                     + w1 * ubuf[base - 1:base - 1 + tile, :]) + w2 * u
            else:
                ys = []
                for c in range(n_chunks):
                    seq = sub * n_chunks + c
                    base = seq * (CHUNK + HIST_ROWS) + HIST_ROWS
                    ubuf[base - HIST_ROWS:base, :] = sin_conv_ref[seq]
                    ubuf[base:base + CHUNK, :] = u[c * CHUNK:(c + 1) * CHUNK]
                    ys.append((w0 * ubuf[base - 2:base - 2 + CHUNK, :]
                               + w1 * ubuf[base - 1:base - 1 + CHUNK, :])
                              + w2 * ubuf[base:base + CHUNK, :])
                    sconv_ref[seq] = ubuf[base + CHUNK - HIST_ROWS:base + CHUNK, :]
                y = jnp.concatenate(ys, axis=0)
            conv_y = _dot((proj(4) * y).astype(BF16), wco_ref[...])
            mix = mix + jax.nn.sigmoid(proj(8)) * conv_y

            h_ref[rows, :] = x + _dot(mix.astype(BF16), wo_ref[...])

        if carry:
            tail = ubuf[subtiles * tile:subtiles * tile + HIST_ROWS, :]
            sconv_ref[...] = tail
            ubuf[0:HIST_ROWS, :] = tail

    if carry:
        n_conv = win_ref.shape[0] // winf_ref.shape[0]

        @pl.when(step < n_conv)
        def _convert():
            for f_ref, b_ref, w_ref in ((winf_ref, winb_ref, win_ref), (wrof_ref, wrob_ref, wro_ref),
                                        (wcof_ref, wcob_ref, wco_ref), (wof_ref, wob_ref, wo_ref)):
                slab = f_ref.shape[0]
                v = f_ref[...].astype(BF16)
                b_ref[...] = v
                w_ref[pl.ds(pl.multiple_of(step * slab, slab), slab), :] = v

        @pl.when(step >= n_conv)
        def _tiles():
            ff1_bf_ref[...] = ff1_ref[...].astype(BF16)
            ff2_bf_ref[...] = ff2_ref[...].astype(BF16)
            tiles(step - n_conv)
    else:
        tiles(step)


def _resident(shape):
    nd = len(shape)
    return pl.BlockSpec(shape, lambda i: (0,) * nd, pipeline_mode=pl.Buffered(1))


def _mixer(x, state_ret, state_conv, n1, gnw, cw, weights, ff=None, *, carry):
    n, d = x.shape
    tile = MIXER_TILE
    subtiles = PROMPT_SUBTILES if carry else SAMPLE_SUBTILES
    rows = tile * subtiles
    n_seqs = rows // CHUNK
    tile_steps = n // rows
    n_conv = CONVERT_STEPS if carry else 0
    grid = (n_conv + tile_steps,)
    tile_idx = lambda i: jnp.maximum(i - n_conv, 0)
    tok = pl.BlockSpec((rows, d), lambda i: (tile_idx(i), 0))
    small = [n1, gnw, cw]
    small_specs = [_resident(w.shape) for w in small]

    def slabs(w, steps, index):
        slab = w.shape[0] // steps
        assert slab * steps == w.shape[0] and slab % BF16_SUBLANES == 0
        return pl.BlockSpec((slab, w.shape[1]), lambda i: (index(i), 0))

    if carry:
        conv_idx = lambda i: jnp.minimum(i, n_conv - 1)
        w_slabs = [slabs(w, n_conv, conv_idx) for w in weights]
        ff_slabs = [slabs(w, tile_steps, tile_idx) for w in ff]
        inputs = [x] + small + list(weights) + list(ff)
        in_specs = [tok] + small_specs + w_slabs + ff_slabs
        out_shape = (jax.ShapeDtypeStruct((n, d), F32),
                     jax.ShapeDtypeStruct((HEADS, HEAD_DIM, HEAD_DIM), F32),
                     jax.ShapeDtypeStruct((HIST_ROWS, d), F32),
                     *[jax.ShapeDtypeStruct(w.shape, BF16) for w in (*weights, *ff)])
        out_specs = (tok,
                     pl.BlockSpec((HEADS, HEAD_DIM, HEAD_DIM), lambda i: (0, 0, 0)),
                     pl.BlockSpec((HIST_ROWS, d), lambda i: (0, 0)),
                     *w_slabs, *ff_slabs)
        conv_rows = rows + HIST_ROWS
    else:
        ret_spec = pl.BlockSpec((n_seqs, HEADS, HEAD_DIM, HEAD_DIM), lambda i: (i, 0, 0, 0))
        conv_spec = pl.BlockSpec((n_seqs, HIST_ROWS, d), lambda i: (i, 0, 0))
        inputs = [x, state_ret, state_conv] + small + list(weights)
        in_specs = [tok, ret_spec, conv_spec] + small_specs + [_resident(w.shape) for w in weights]
        out_shape = (jax.ShapeDtypeStruct((n, d), F32),
                     jax.ShapeDtypeStruct(state_ret.shape, F32),
                     jax.ShapeDtypeStruct(state_conv.shape, F32))
        out_specs = (tok, ret_spec, conv_spec)
        conv_rows = n_seqs * (CHUNK + HIST_ROWS)
    scratch = [
        pltpu.VMEM((HEADS, tile, tile), F32),
        pltpu.VMEM((HEADS, tile, HEAD_DIM), F32),
        pltpu.VMEM((HEADS, tile, HEAD_DIM), F32),
        pltpu.VMEM((subtiles, tile, HEAD_DIM), F32),
        pltpu.VMEM((subtiles, tile, HEAD_DIM), F32),
        pltpu.VMEM((subtiles, tile, d), BF16),
        pltpu.VMEM((subtiles, tile, d), BF16),
        pltpu.VMEM((subtiles, tile, d), BF16),
        pltpu.VMEM((subtiles, tile, d), BF16),
        pltpu.VMEM((subtiles, tile, d), BF16),
        pltpu.VMEM((subtiles, tile, d), F32),
        pltpu.VMEM((subtiles, tile, d), BF16),
        pltpu.VMEM((conv_rows, d), F32),
    ]
    if carry:
        scratch += [pltpu.VMEM((tile, HEAD_DIM), F32)] * 2
        scratch += [pltpu.VMEM(w.shape, BF16) for w in weights]
    return pl.pallas_call(
        functools.partial(_mixer_kernel, carry, tile, subtiles),
        out_shape=out_shape,
        grid=grid,
        in_specs=in_specs,
        out_specs=out_specs,
        scratch_shapes=scratch,
        compiler_params=pltpu.CompilerParams(
            dimension_semantics=("arbitrary",), vmem_limit_bytes=VMEM_LIMIT_BYTES),
        name="mixer_prompt" if carry else "mixer_sample",
    )(*inputs)


def _mlp_kernel(tile, subtiles, h_ref, n2_ref, w1_ref, w2_ref, nf_ref, y_ref):
    for sub in range(subtiles):
        rows = slice(sub * tile, (sub + 1) * tile)
        h = h_ref[rows, :]
        hn = _rms(h, n2_ref[...]).astype(BF16)
        a = jnp.maximum(_dot(hn, w1_ref[...]), 0.0)
        a2 = (a * a).astype(BF16)
        half = tile // 2
        for r in range(2):
            rs = slice(r * half, (r + 1) * half)
            o = h[rs] + _dot(a2[rs], w2_ref[...])
            y_ref[sub * tile + r * half:sub * tile + (r + 1) * half, :] = _rms(o, nf_ref[...])


def _mlp(h, n2, w1, w2, nf, name):
    n, d = h.shape
    tile, subtiles = MLP_TILE, MLP_SUBTILES
    tok = pl.BlockSpec((tile * subtiles, d), lambda i: (i, 0))
    weights = [n2, w1, w2, nf]
    return pl.pallas_call(
        functools.partial(_mlp_kernel, tile, subtiles),
        out_shape=jax.ShapeDtypeStruct((n, d), F32),
        grid=(n // (tile * subtiles),),
        in_specs=[tok] + [_resident(w.shape) for w in weights],
        out_specs=tok,
        compiler_params=pltpu.CompilerParams(
            dimension_semantics=("arbitrary",), vmem_limit_bytes=VMEM_LIMIT_BYTES),
        name=name,
    )(h, *weights)


def kernel(x_prompt, x_sample, state_ret, state_conv, norm1, w_in, ret_gn_w, conv_w, w_ret_out, w_conv_out, w_o, norm2, w_ff1, w_ff2, norm_f):
    depth = w_in.shape[0]
    assert depth == 1, "single-layer kernel"
    bp, lp, d = x_prompt.shape
    bs, ls, _ = x_sample.shape
    assert bp == 1 and ls == CHUNK
    assert lp % (MIXER_TILE * PROMPT_SUBTILES) == 0 and (bs * ls) % (MIXER_TILE * SAMPLE_SUBTILES) == 0
    assert lp % (MLP_TILE * MLP_SUBTILES) == 0 and (bs * ls) % (MLP_TILE * MLP_SUBTILES) == 0

    row = lambda v: v.reshape(1, -1)
    small = (row(norm1[0]), row(ret_gn_w[0]), conv_w[0])

    hp, ret_p, conv_p, *bf = _mixer(
        x_prompt.reshape(lp, d), None, None, *small,
        (w_in[0], w_ret_out[0], w_conv_out[0], w_o[0]), ff=(w_ff1[0], w_ff2[0]), carry=True)
    mixer_bf, (ff1_bf, ff2_bf) = bf[:4], bf[4:]
    mlp_w = (row(norm2[0]), ff1_bf, ff2_bf, row(norm_f))
    conv_hist = jnp.pad(state_conv[0], ((0, 0), (HIST_ROWS - (CONV_W - 1), 0), (0, 0)))
    hs, ret_s, conv_s = _mixer(x_sample.reshape(bs * ls, d), state_ret[0], conv_hist, *small, mixer_bf, carry=False)

    yp = _mlp(hp, *mlp_w, name="mlp_prompt").reshape(bp, lp, d)
    ys = _mlp(hs, *mlp_w, name="mlp_sample").reshape(bs, ls, d)
    keep = slice(HIST_ROWS - (CONV_W - 1), HIST_ROWS)
    return (yp, ys, ret_p[None, None], conv_p[None, None, keep],
            ret_s[None], conv_s[None, :, keep])
```

```python
import functools
import math

import jax
import jax.numpy as jnp
from jax import lax
from jax.experimental import pallas as pl
from jax.experimental.pallas import tpu as pltpu

CHUNK = 64
CHUNK_SHIFT = 6
HEADS = 8
HEAD_DIM = 128
CONV_W = 3
ROPE_BASE = 10000.0
EPS = 1e-6
PAST_LEN = 4096
HIST_ROWS = 8
BF16_SUBLANES = 16

MIXER_TILE = 256
PROMPT_SUBTILES = 2
SAMPLE_SUBTILES = 1
MLP_TILE = 512
MLP_SUBTILES = 2
MLP_SAMPLE_SUBTILES = 1
CONVERT_STEPS = 16
VMEM_LIMIT_BYTES = 60 * 1024 * 1024

F32 = jnp.float32
BF16 = jnp.bfloat16


def _log_gamma(h):
    return math.log(1.0 - 2.0 ** (-5.0 - h))


def _rms(x, w):
    ms = jnp.mean(x * x, axis=-1, keepdims=True)
    return (x * lax.rsqrt(ms + EPS)) * w


def _dot(a, b):
    return jnp.dot(a, b, preferred_element_type=F32)


def _dot_nt(a, b):
    return lax.dot_general(a, b, (((1,), (1,)), ((), ())), preferred_element_type=F32)


def _dot_tn(a, b):
    return lax.dot_general(a, b, (((0,), (0,)), ((), ())), preferred_element_type=F32)


def _rope_cos_sin(pos):
    lane = lax.broadcasted_iota(jnp.int32, pos.shape, 1)
    half = HEAD_DIM // 2
    inv = jnp.exp((-math.log(ROPE_BASE)) * (lane & (half - 1)).astype(F32) / half)
    ang = pos * inv
    return jnp.cos(ang), jnp.where(lane < half, -1.0, 1.0) * jnp.sin(ang)


def _mixer_kernel(carry, tile, subtiles, *refs):
    n_chunks = tile // CHUNK
    if carry:
        (x_ref, n1_ref, gnw_ref, cw_ref, winf_ref, wrof_ref, wcof_ref, wof_ref, ff1_ref, ff2_ref,
         h_ref, sret_ref, sconv_ref, winb_ref, wrob_ref, wcob_ref, wob_ref, ff1_bf_ref, ff2_bf_ref,
         dmat, rowdec, coldec, cos_s, sin_s, xn_s, q_s, k_s, kd_s, v_s, sg_s, gated_s, ubuf,
         cosb_s, sinb_s, win_ref, wro_ref, wco_ref, wo_ref) = refs
        sin_ret_ref = sin_conv_ref = None
    else:
        (x_ref, sin_ret_ref, sin_conv_ref, n1_ref, gnw_ref, cw_ref, win_ref, wro_ref, wco_ref, wo_ref,
         h_ref, sret_ref, sconv_ref,
         dmat, rowdec, coldec, cos_s, sin_s, xn_s, q_s, k_s, kd_s, v_s, sg_s, gated_s, ubuf) = refs
    d_model = x_ref.shape[-1]
    step = pl.program_id(0)
    state_len = tile if carry else CHUNK

    @pl.when(step == 0)
    def _init():
        ii = lax.broadcasted_iota(jnp.int32, (tile, tile), 0)
        jj = lax.broadcasted_iota(jnp.int32, (tile, tile), 1)
        dist = jnp.abs(ii - jj).astype(F32)
        ci, cj = ii >> CHUNK_SHIFT, jj >> CHUNK_SHIFT
        keep = (cj <= ci) if carry else (cj == ci)
        row = lax.broadcasted_iota(jnp.int32, (tile, HEAD_DIM), 0)
        loc = (row if carry else row & (CHUNK - 1)).astype(F32)
        for h in range(HEADS):
            lg = _log_gamma(h)
            dmat[h] = jnp.where(keep, jnp.exp(lg * dist), 0.0)
            rowdec[h] = jnp.exp(lg * (loc + 1.0))
            coldec[h] = jnp.exp(lg * (state_len - 1.0 - loc))
        if carry:
            cosb_s[...], sinb_s[...] = _rope_cos_sin(row.astype(F32))
            sret_ref[...] = jnp.zeros_like(sret_ref)
            ubuf[0:HIST_ROWS, :] = jnp.zeros((HIST_ROWS, d_model), F32)
        else:
            cos, sin = _rope_cos_sin((PAST_LEN + (row & (CHUNK - 1))).astype(F32))
            for sub in range(subtiles):
                cos_s[sub], sin_s[sub] = cos, sin

    def tiles(t):
        w0, w1, w2 = cw_ref[0:1, :], cw_ref[1:2, :], cw_ref[2:3, :]

        for sub in range(subtiles):
            rows = slice(sub * tile, (sub + 1) * tile)
            x = x_ref[rows, :]
            xn_s[sub] = _rms(x, n1_ref[...]).astype(BF16)

            def proj(g):
                return _dot(xn_s[sub], win_ref[:, g * d_model:(g + 1) * d_model])

            if carry:
                first = jnp.zeros((8, HEAD_DIM), jnp.int32) + (t * subtiles + sub) * tile
                ct, st = _rope_cos_sin(first.astype(F32))
                ct, st = ct[0:1, :], st[0:1, :]
                cb, sb = cosb_s[...], sinb_s[...]
                cos_s[sub] = cb * ct - sb * st
                sin_s[sub] = sb * ct + cb * st

            def rope(z):
                return z * cos_s[sub] + pltpu.roll(z, HEAD_DIM // 2, 1) * sin_s[sub]

            q = proj(0)
            for h in range(HEADS):
                hs = slice(h * HEAD_DIM, (h + 1) * HEAD_DIM)
                q_s[sub, :, hs] = rope(q[:, hs]).astype(BF16)
            k = proj(1)
            for h in range(HEADS):
                hs = slice(h * HEAD_DIM, (h + 1) * HEAD_DIM)
                kr = rope(k[:, hs]) * (HEAD_DIM ** -0.5)
                k_s[sub, :, hs] = kr.astype(BF16)
                kd_s[sub, :, hs] = (kr * coldec[h]).astype(BF16)
            v_s[sub] = proj(2).astype(BF16)

            def scores(h):
                hs = slice(h * HEAD_DIM, (h + 1) * HEAD_DIM)
                return (_dot_nt(q_s[sub, :, hs], k_s[sub, :, hs]) * dmat[h]).astype(BF16)

            p_next = scores(0)
            sg_s[sub] = jax.nn.silu(proj(3))
            for h in range(HEADS):
                hs = slice(h * HEAD_DIM, (h + 1) * HEAD_DIM)
                decay = math.exp(_log_gamma(h) * state_len)
                qh, kdh, vh = q_s[sub, :, hs], kd_s[sub, :, hs], v_s[sub, :, hs]
                p = p_next
                if h + 1 < HEADS:
                    p_next = scores(h + 1)
                o = _dot(p, vh)
                if carry:
                    s0 = sret_ref[h]
                    o = o + _dot(qh, s0.astype(BF16)) * rowdec[h]
                    sret_ref[h] = s0 * decay + _dot_tn(kdh, vh)
                else:
                    inter = []
                    for c in range(n_chunks):
                        rs = slice(c * CHUNK, (c + 1) * CHUNK)
                        seq = sub * n_chunks + c
                        s0 = sin_ret_ref[seq, h]
                        inter.append(_dot(qh[rs], s0.astype(BF16)))
                        sret_ref[seq, h] = s0 * decay + _dot_tn(kdh[rs], vh[rs])
                    o = o + jnp.concatenate(inter, axis=0) * rowdec[h]
                mu = jnp.mean(o, axis=-1, keepdims=True)
                d = o - mu
                var = jnp.mean(d * d, axis=-1, keepdims=True)
                on = (d * lax.rsqrt(var + EPS)) * gnw_ref[:, hs]
                gated_s[sub, :, hs] = (on * sg_s[sub, :, hs]).astype(BF16)

            u = proj(5) * proj(6)
            ret_y = _dot(gated_s[sub], wro_ref[...])
            mix = jax.nn.sigmoid(proj(7)) * ret_y
            if carry:
                base = HIST_ROWS + sub * tile
                ubuf[base:base + tile, :] = u
                y = (w0 * ubuf[base - 2:base - 2 + tile, :]
                     + w1 * ubuf[base - 1:base - 1 + tile, :]) + w2 * u
            else:
                ys = []
                for c in range(n_chunks):
                    seq = sub * n_chunks + c
                    base = seq * (CHUNK + HIST_ROWS) + HIST_ROWS
                    ubuf[base - HIST_ROWS:base, :] = sin_conv_ref[seq]
                    ubuf[base:base + CHUNK, :] = u[c * CHUNK:(c + 1) * CHUNK]
                    ys.append((w0 * ubuf[base - 2:base - 2 + CHUNK, :]
                               + w1 * ubuf[base - 1:base - 1 + CHUNK, :])
                              + w2 * ubuf[base:base + CHUNK, :])
                    sconv_ref[seq] = ubuf[base + CHUNK - HIST_ROWS:base + CHUNK, :]
                y = jnp.concatenate(ys, axis=0)
            conv_y = _dot((proj(4) * y).astype(BF16), wco_ref[...])
            mix = mix + jax.nn.sigmoid(proj(8)) * conv_y

            h_ref[rows, :] = x + _dot(mix.astype(BF16), wo_ref[...])

        if carry:
            tail = ubuf[subtiles * tile:subtiles * tile + HIST_ROWS, :]
            sconv_ref[...] = tail
            ubuf[0:HIST_ROWS, :] = tail

    if carry:
        n_conv = win_ref.shape[0] // winf_ref.shape[0]

        @pl.when(step < n_conv)
        def _convert():
            for f_ref, b_ref, w_ref in ((winf_ref, winb_ref, win_ref), (wrof_ref, wrob_ref, wro_ref),
                                        (wcof_ref, wcob_ref, wco_ref), (wof_ref, wob_ref, wo_ref)):
                slab = f_ref.shape[0]
                v = f_ref[...].astype(BF16)
                b_ref[...] = v
                w_ref[pl.ds(pl.multiple_of(step * slab, slab), slab), :] = v

        @pl.when(step >= n_conv)
        def _tiles():
            ff1_bf_ref[...] = ff1_ref[...].astype(BF16)
            ff2_bf_ref[...] = ff2_ref[...].astype(BF16)
            tiles(step - n_conv)
    else:
        tiles(step)


def _resident(shape):
    nd = len(shape)
    return pl.BlockSpec(shape, lambda i: (0,) * nd, pipeline_mode=pl.Buffered(1))


def _mixer(x, state_ret, state_conv, n1, gnw, cw, weights, ff=None, *, carry):
    n, d = x.shape
    tile = MIXER_TILE
    subtiles = PROMPT_SUBTILES if carry else SAMPLE_SUBTILES
    rows = tile * subtiles
    n_seqs = rows // CHUNK
    tile_steps = n // rows
    n_conv = CONVERT_STEPS if carry else 0
    grid = (n_conv + tile_steps,)
    tile_idx = lambda i: jnp.maximum(i - n_conv, 0)
    tok = pl.BlockSpec((rows, d), lambda i: (tile_idx(i), 0))
    small = [n1, gnw, cw]
    small_specs = [_resident(w.shape) for w in small]

    def slabs(w, steps, index):
        slab = w.shape[0] // steps
        assert slab * steps == w.shape[0] and slab % BF16_SUBLANES == 0
        return pl.BlockSpec((slab, w.shape[1]), lambda i: (index(i), 0))

    if carry:
        conv_idx = lambda i: jnp.minimum(i, n_conv - 1)
        w_slabs = [slabs(w, n_conv, conv_idx) for w in weights]
        ff_slabs = [slabs(w, tile_steps, tile_idx) for w in ff]
        inputs = [x] + small + list(weights) + list(ff)
        in_specs = [tok] + small_specs + w_slabs + ff_slabs
        out_shape = (jax.ShapeDtypeStruct((n, d), F32),
                     jax.ShapeDtypeStruct((HEADS, HEAD_DIM, HEAD_DIM), F32),
                     jax.ShapeDtypeStruct((HIST_ROWS, d), F32),
                     *[jax.ShapeDtypeStruct(w.shape, BF16) for w in (*weights, *ff)])
        out_specs = (tok,
                     pl.BlockSpec((HEADS, HEAD_DIM, HEAD_DIM), lambda i: (0, 0, 0)),
                     pl.BlockSpec((HIST_ROWS, d), lambda i: (0, 0)),
                     *w_slabs, *ff_slabs)
        conv_rows = rows + HIST_ROWS
    else:
        ret_spec = pl.BlockSpec((n_seqs, HEADS, HEAD_DIM, HEAD_DIM), lambda i: (i, 0, 0, 0))
        conv_spec = pl.BlockSpec((n_seqs, HIST_ROWS, d), lambda i: (i, 0, 0))
        inputs = [x, state_ret, state_conv] + small + list(weights)
        in_specs = [tok, ret_spec, conv_spec] + small_specs + [_resident(w.shape) for w in weights]
        out_shape = (jax.ShapeDtypeStruct((n, d), F32),
                     jax.ShapeDtypeStruct(state_ret.shape, F32),
                     jax.ShapeDtypeStruct(state_conv.shape, F32))
        out_specs = (tok, ret_spec, conv_spec)
        conv_rows = n_seqs * (CHUNK + HIST_ROWS)
    scratch = [
        pltpu.VMEM((HEADS, tile, tile), F32),
        pltpu.VMEM((HEADS, tile, HEAD_DIM), F32),
        pltpu.VMEM((HEADS, tile, HEAD_DIM), F32),
        pltpu.VMEM((subtiles, tile, HEAD_DIM), F32),
        pltpu.VMEM((subtiles, tile, HEAD_DIM), F32),
        pltpu.VMEM((subtiles, tile, d), BF16),
        pltpu.VMEM((subtiles, tile, d), BF16),
        pltpu.VMEM((subtiles, tile, d), BF16),
        pltpu.VMEM((subtiles, tile, d), BF16),
        pltpu.VMEM((subtiles, tile, d), BF16),
        pltpu.VMEM((subtiles, tile, d), F32),
        pltpu.VMEM((subtiles, tile, d), BF16),
        pltpu.VMEM((conv_rows, d), F32),
    ]
    if carry:
        scratch += [pltpu.VMEM((tile, HEAD_DIM), F32)] * 2
        scratch += [pltpu.VMEM(w.shape, BF16) for w in weights]
    return pl.pallas_call(
        functools.partial(_mixer_kernel, carry, tile, subtiles),
        out_shape=out_shape,
        grid=grid,
        in_specs=in_specs,
        out_specs=out_specs,
        scratch_shapes=scratch,
        compiler_params=pltpu.CompilerParams(
            dimension_semantics=("arbitrary",), vmem_limit_bytes=VMEM_LIMIT_BYTES),
        name="mixer_prompt" if carry else "mixer_sample",
    )(*inputs)


def _mlp_kernel(tile, steps_a, subtiles_a, subtiles_b, ha_ref, hb_ref, n2_ref, w1_ref, w2_ref, nf_ref,
                ya_ref, yb_ref):
    def passes(h_ref, y_ref, subtiles):
        for sub in range(subtiles):
            rows = slice(sub * tile, (sub + 1) * tile)
            h = h_ref[rows, :]
            hn = _rms(h, n2_ref[...]).astype(BF16)
            a = jnp.maximum(_dot(hn, w1_ref[...]), 0.0)
            a2 = (a * a).astype(BF16)
            half = tile // 2
            for r in range(2):
                rs = slice(r * half, (r + 1) * half)
                o = h[rs] + _dot(a2[rs], w2_ref[...])
                y_ref[sub * tile + r * half:sub * tile + (r + 1) * half, :] = _rms(o, nf_ref[...])

    step = pl.program_id(0)

    @pl.when(step < steps_a)
    def _a():
        passes(ha_ref, ya_ref, subtiles_a)

    @pl.when(step >= steps_a)
    def _b():
        passes(hb_ref, yb_ref, subtiles_b)


def _mlp(ha, hb, n2, w1, w2, nf):
    d = ha.shape[1]
    tile = MLP_TILE
    rows_a, rows_b = tile * MLP_SUBTILES, tile * MLP_SAMPLE_SUBTILES
    steps_a, steps_b = ha.shape[0] // rows_a, hb.shape[0] // rows_b
    spec_a = pl.BlockSpec((rows_a, d), lambda i: (jnp.minimum(i, steps_a - 1), 0))
    spec_b = pl.BlockSpec((rows_b, d), lambda i: (jnp.maximum(i - steps_a, 0), 0))
    weights = [n2, w1, w2, nf]
    return pl.pallas_call(
        functools.partial(_mlp_kernel, tile, steps_a, MLP_SUBTILES, MLP_SAMPLE_SUBTILES),
        out_shape=(jax.ShapeDtypeStruct(ha.shape, F32), jax.ShapeDtypeStruct(hb.shape, F32)),
        grid=(steps_a + steps_b,),
        in_specs=[spec_a, spec_b] + [_resident(w.shape) for w in weights],
        out_specs=(spec_a, spec_b),
        compiler_params=pltpu.CompilerParams(
            dimension_semantics=("arbitrary",), vmem_limit_bytes=VMEM_LIMIT_BYTES),
        name="mlp",
    )(ha, hb, *weights)


def kernel(x_prompt, x_sample, state_ret, state_conv, norm1, w_in, ret_gn_w, conv_w, w_ret_out, w_conv_out, w_o, norm2, w_ff1, w_ff2, norm_f):
    depth = w_in.shape[0]
    assert depth == 1, "single-layer kernel"
    bp, lp, d = x_prompt.shape
    bs, ls, _ = x_sample.shape
    assert bp == 1 and ls == CHUNK
    assert lp % (MIXER_TILE * PROMPT_SUBTILES) == 0 and (bs * ls) % (MIXER_TILE * SAMPLE_SUBTILES) == 0
    assert lp % (MLP_TILE * MLP_SUBTILES) == 0 and (bs * ls) % (MLP_TILE * MLP_SAMPLE_SUBTILES) == 0

    row = lambda v: v.reshape(1, -1)
    small = (row(norm1[0]), row(ret_gn_w[0]), conv_w[0])

    hp, ret_p, conv_p, *bf = _mixer(
        x_prompt.reshape(lp, d), None, None, *small,
        (w_in[0], w_ret_out[0], w_conv_out[0], w_o[0]), ff=(w_ff1[0], w_ff2[0]), carry=True)
    mixer_bf, (ff1_bf, ff2_bf) = bf[:4], bf[4:]
    mlp_w = (row(norm2[0]), ff1_bf, ff2_bf, row(norm_f))
    conv_hist = jnp.pad(state_conv[0], ((0, 0), (HIST_ROWS - (CONV_W - 1), 0), (0, 0)))
    hs, ret_s, conv_s = _mixer(x_sample.reshape(bs * ls, d), state_ret[0], conv_hist, *small, mixer_bf, carry=False)

    yp, ys = _mlp(hp, hs, *mlp_w)
    yp, ys = yp.reshape(bp, lp, d), ys.reshape(bs, ls, d)
    keep = slice(HIST_ROWS - (CONV_W - 1), HIST_ROWS)
    return (yp, ys, ret_p[None, None], conv_p[None, None, keep],
            ret_s[None], conv_s[None, :, keep])
```

```python
import functools
import math

import jax
import jax.numpy as jnp
from jax import lax
from jax.experimental import pallas as pl
from jax.experimental.pallas import tpu as pltpu

CHUNK = 64
CHUNK_SHIFT = 6
HEADS = 8
HEAD_DIM = 128
CONV_W = 3
ROPE_BASE = 10000.0
EPS = 1e-6
PAST_LEN = 4096
HIST_ROWS = 8
BF16_SUBLANES = 16
F32_SUBLANES = 8

MIXER_TILE = 256
PROMPT_SUBTILES = 2
SAMPLE_SUBTILES = 1
MLP_TILE = 512
MLP_SUBTILES = 2
CONVERT_STEPS = 16
VMEM_LIMIT_BYTES = 60 * 1024 * 1024

F32 = jnp.float32
BF16 = jnp.bfloat16


def _log_gamma(h):
    return math.log(1.0 - 2.0 ** (-5.0 - h))


def _rms(x, w):
    ms = jnp.mean(x * x, axis=-1, keepdims=True)
    return (x * lax.rsqrt(ms + EPS)) * w


def _dot(a, b):
    return jnp.dot(a, b, preferred_element_type=F32)


def _dot_nt(a, b):
    return lax.dot_general(a, b, (((1,), (1,)), ((), ())), preferred_element_type=F32)


def _dot_tn(a, b):
    return lax.dot_general(a, b, (((0,), (0,)), ((), ())), preferred_element_type=F32)


def _rope_cos_sin(pos):
    lane = lax.broadcasted_iota(jnp.int32, pos.shape, 1)
    half = HEAD_DIM // 2
    inv = jnp.exp((-math.log(ROPE_BASE)) * (lane & (half - 1)).astype(F32) / half)
    ang = pos * inv
    return jnp.cos(ang), jnp.where(lane < half, -1.0, 1.0) * jnp.sin(ang)


def _mixer_kernel(carry, tile, subtiles, *refs):
    n_chunks = tile // CHUNK
    if carry:
        (x_ref, n1_ref, gnw_ref, cw_ref, winf_ref, wrof_ref, wcof_ref, wof_ref, ff1_ref, ff2_ref,
         h_ref, sret_ref, sconv_ref, winb_ref, wrob_ref, wcob_ref, wob_ref, ff1_bf_ref, ff2_bf_ref,
         dmat, rowdec, coldec, cos_s, sin_s, xn_s, q_s, k_s, kd_s, v_s, sg_s, gated_s, ubuf,
         cosb_s, sinb_s, win_ref, wro_ref, wco_ref, wo_ref) = refs
        sin_ret_ref = sin_conv_ref = None
    else:
        (x_ref, sin_ret_ref, sin_conv_ref, n1_ref, gnw_ref, cw_ref, win_ref, wro_ref, wco_ref, wo_ref,
         h_ref, sret_ref, sconv_ref,
         dmat, rowdec, coldec, cos_s, sin_s, xn_s, q_s, k_s, kd_s, v_s, sg_s, gated_s, ubuf) = refs
    d_model = x_ref.shape[-1]
    step = pl.program_id(0)
    state_len = tile if carry else CHUNK

    @pl.when(step == 0)
    def _init():
        ii = lax.broadcasted_iota(jnp.int32, (tile, tile), 0)
        jj = lax.broadcasted_iota(jnp.int32, (tile, tile), 1)
        dist = jnp.abs(ii - jj).astype(F32)
        ci, cj = ii >> CHUNK_SHIFT, jj >> CHUNK_SHIFT
        keep = (cj <= ci) if carry else (cj == ci)
        row = lax.broadcasted_iota(jnp.int32, (tile, HEAD_DIM), 0)
        loc = (row if carry else row & (CHUNK - 1)).astype(F32)
        for h in range(HEADS):
            lg = _log_gamma(h)
            dmat[h] = jnp.where(keep, jnp.exp(lg * dist), 0.0)
            rowdec[h] = jnp.exp(lg * (loc + 1.0))
            coldec[h] = jnp.exp(lg * (state_len - 1.0 - loc))
        if carry:
            cosb_s[...], sinb_s[...] = _rope_cos_sin(row.astype(F32))
            sret_ref[...] = jnp.zeros_like(sret_ref)
            ubuf[0:HIST_ROWS, :] = jnp.zeros((HIST_ROWS, d_model), F32)
        else:
            cos, sin = _rope_cos_sin((PAST_LEN + (row & (CHUNK - 1))).astype(F32))
            for sub in range(subtiles):
                cos_s[sub], sin_s[sub] = cos, sin

    def tiles(t):
        w0, w1, w2 = cw_ref[0:1, :], cw_ref[1:2, :], cw_ref[2:3, :]

        for sub in range(subtiles):
            rows = slice(sub * tile, (sub + 1) * tile)
            x = x_ref[rows, :]
            xn_s[sub] = _rms(x, n1_ref[0:1, :]).astype(BF16)

            def proj(g):
                return _dot(xn_s[sub], win_ref[:, g * d_model:(g + 1) * d_model])

            if carry:
                first = jnp.zeros((8, HEAD_DIM), jnp.int32) + (t * subtiles + sub) * tile
                ct, st = _rope_cos_sin(first.astype(F32))
                ct, st = ct[0:1, :], st[0:1, :]
                cb, sb = cosb_s[...], sinb_s[...]
                cos_s[sub] = cb * ct - sb * st
                sin_s[sub] = sb * ct + cb * st

            def rope(z):
                return z * cos_s[sub] + pltpu.roll(z, HEAD_DIM // 2, 1) * sin_s[sub]

            q = proj(0)
            for h in range(HEADS):
                hs = slice(h * HEAD_DIM, (h + 1) * HEAD_DIM)
                q_s[sub, :, hs] = rope(q[:, hs]).astype(BF16)
            k = proj(1)
            for h in range(HEADS):
                hs = slice(h * HEAD_DIM, (h + 1) * HEAD_DIM)
                kr = rope(k[:, hs]) * (HEAD_DIM ** -0.5)
                k_s[sub, :, hs] = kr.astype(BF16)
                kd_s[sub, :, hs] = (kr * coldec[h]).astype(BF16)
            v_s[sub] = proj(2).astype(BF16)

            def scores(h):
                hs = slice(h * HEAD_DIM, (h + 1) * HEAD_DIM)
                return (_dot_nt(q_s[sub, :, hs], k_s[sub, :, hs]) * dmat[h]).astype(BF16)

            p_next = scores(0)
            sg_s[sub] = jax.nn.silu(proj(3))
            for h in range(HEADS):
                hs = slice(h * HEAD_DIM, (h + 1) * HEAD_DIM)
                decay = math.exp(_log_gamma(h) * state_len)
                qh, kdh, vh = q_s[sub, :, hs], kd_s[sub, :, hs], v_s[sub, :, hs]
                p = p_next
                if h + 1 < HEADS:
                    p_next = scores(h + 1)
                o = _dot(p, vh)
                if carry:
                    s0 = sret_ref[h]
                    o = o + _dot(qh, s0.astype(BF16)) * rowdec[h]
                    sret_ref[h] = s0 * decay + _dot_tn(kdh, vh)
                else:
                    inter = []
                    for c in range(n_chunks):
                        rs = slice(c * CHUNK, (c + 1) * CHUNK)
                        seq = sub * n_chunks + c
                        s0 = sin_ret_ref[seq, h]
                        inter.append(_dot(qh[rs], s0.astype(BF16)))
                        sret_ref[seq, h] = s0 * decay + _dot_tn(kdh[rs], vh[rs])
                    o = o + jnp.concatenate(inter, axis=0) * rowdec[h]
                mu = jnp.mean(o, axis=-1, keepdims=True)
                d = o - mu
                var = jnp.mean(d * d, axis=-1, keepdims=True)
                on = (d * lax.rsqrt(var + EPS)) * gnw_ref[0:1, hs]
                gated_s[sub, :, hs] = (on * sg_s[sub, :, hs]).astype(BF16)

            u = proj(5) * proj(6)
            ret_y = _dot(gated_s[sub], wro_ref[...])
            mix = jax.nn.sigmoid(proj(7)) * ret_y
            if carry:
                base = HIST_ROWS + sub * tile
                ubuf[base:base + tile, :] = u
                y = (w0 * ubuf[base - 2:base - 2 + tile, :]
                     + w1 * ubuf[base - 1:base - 1 + tile, :]) + w2 * u
            else:
                ys = []
                for c in range(n_chunks):
                    seq = sub * n_chunks + c
                    base = seq * (CHUNK + HIST_ROWS) + HIST_ROWS
                    ubuf[base - HIST_ROWS:base, :] = sin_conv_ref[seq]
                    ubuf[base:base + CHUNK, :] = u[c * CHUNK:(c + 1) * CHUNK]
                    ys.append((w0 * ubuf[base - 2:base - 2 + CHUNK, :]
                               + w1 * ubuf[base - 1:base - 1 + CHUNK, :])
                              + w2 * ubuf[base:base + CHUNK, :])
                    sconv_ref[seq] = ubuf[base + CHUNK - HIST_ROWS:base + CHUNK, :]
                y = jnp.concatenate(ys, axis=0)
            conv_y = _dot((proj(4) * y).astype(BF16), wco_ref[...])
            mix = mix + jax.nn.sigmoid(proj(8)) * conv_y

            h_ref[rows, :] = x + _dot(mix.astype(BF16), wo_ref[...])

        if carry:
            tail = ubuf[subtiles * tile:subtiles * tile + HIST_ROWS, :]
            sconv_ref[...] = tail
            ubuf[0:HIST_ROWS, :] = tail

    if carry:
        n_conv = win_ref.shape[0] // winf_ref.shape[0]

        @pl.when(step < n_conv)
        def _convert():
            for f_ref, b_ref, w_ref in ((winf_ref, winb_ref, win_ref), (wrof_ref, wrob_ref, wro_ref),
                                        (wcof_ref, wcob_ref, wco_ref), (wof_ref, wob_ref, wo_ref)):
                slab = f_ref.shape[0]
                v = f_ref[...].astype(BF16)
                b_ref[...] = v
                w_ref[pl.ds(pl.multiple_of(step * slab, slab), slab), :] = v

        @pl.when(step >= n_conv)
        def _tiles():
            ff1_bf_ref[...] = ff1_ref[...].astype(BF16)
            ff2_bf_ref[...] = ff2_ref[...].astype(BF16)
            tiles(step - n_conv)
    else:
        tiles(step)


def _resident(shape):
    nd = len(shape)
    return pl.BlockSpec(shape, lambda i: (0,) * nd, pipeline_mode=pl.Buffered(1))


def _mixer(x, state_ret, state_conv, n1, gnw, cw, weights, ff=None, *, carry):
    n, d = x.shape
    tile = MIXER_TILE
    subtiles = PROMPT_SUBTILES if carry else SAMPLE_SUBTILES
    rows = tile * subtiles
    n_seqs = rows // CHUNK
    tile_steps = n // rows
    n_conv = CONVERT_STEPS if carry else 0
    grid = (n_conv + tile_steps,)
    tile_idx = lambda i: jnp.maximum(i - n_conv, 0)
    tok = pl.BlockSpec((rows, d), lambda i: (tile_idx(i), 0))
    small = [n1, gnw, cw]
    small_specs = [_resident(w.shape) for w in small]

    def slabs(w, steps, index):
        slab = w.shape[0] // steps
        assert slab * steps == w.shape[0] and slab % BF16_SUBLANES == 0
        return pl.BlockSpec((slab, w.shape[1]), lambda i: (index(i), 0))

    if carry:
        conv_idx = lambda i: jnp.minimum(i, n_conv - 1)
        w_slabs = [slabs(w, n_conv, conv_idx) for w in weights]
        ff_slabs = [slabs(w, tile_steps, tile_idx) for w in ff]
        inputs = [x] + small + list(weights) + list(ff)
        in_specs = [tok] + small_specs + w_slabs + ff_slabs
        out_shape = (jax.ShapeDtypeStruct((n, d), F32),
                     jax.ShapeDtypeStruct((HEADS, HEAD_DIM, HEAD_DIM), F32),
                     jax.ShapeDtypeStruct((HIST_ROWS, d), F32),
                     *[jax.ShapeDtypeStruct(w.shape, BF16) for w in (*weights, *ff)])
        out_specs = (tok,
                     pl.BlockSpec((HEADS, HEAD_DIM, HEAD_DIM), lambda i: (0, 0, 0)),
                     pl.BlockSpec((HIST_ROWS, d), lambda i: (0, 0)),
                     *w_slabs, *ff_slabs)
        conv_rows = rows + HIST_ROWS
    else:
        ret_spec = pl.BlockSpec((n_seqs, HEADS, HEAD_DIM, HEAD_DIM), lambda i: (i, 0, 0, 0))
        conv_spec = pl.BlockSpec((n_seqs, HIST_ROWS, d), lambda i: (i, 0, 0))
        inputs = [x, state_ret, state_conv] + small + list(weights)
        in_specs = [tok, ret_spec, conv_spec] + small_specs + [_resident(w.shape) for w in weights]
        out_shape = (jax.ShapeDtypeStruct((n, d), F32),
                     jax.ShapeDtypeStruct(state_ret.shape, F32),
                     jax.ShapeDtypeStruct(state_conv.shape, F32))
        out_specs = (tok, ret_spec, conv_spec)
        conv_rows = n_seqs * (CHUNK + HIST_ROWS)
    scratch = [
        pltpu.VMEM((HEADS, tile, tile), F32),
        pltpu.VMEM((HEADS, tile, HEAD_DIM), F32),
        pltpu.VMEM((HEADS, tile, HEAD_DIM), F32),
        pltpu.VMEM((subtiles, tile, HEAD_DIM), F32),
        pltpu.VMEM((subtiles, tile, HEAD_DIM), F32),
        pltpu.VMEM((subtiles, tile, d), BF16),
        pltpu.VMEM((subtiles, tile, d), BF16),
        pltpu.VMEM((subtiles, tile, d), BF16),
        pltpu.VMEM((subtiles, tile, d), BF16),
        pltpu.VMEM((subtiles, tile, d), BF16),
        pltpu.VMEM((subtiles, tile, d), F32),
        pltpu.VMEM((subtiles, tile, d), BF16),
        pltpu.VMEM((conv_rows, d), F32),
    ]
    if carry:
        scratch += [pltpu.VMEM((tile, HEAD_DIM), F32)] * 2
        scratch += [pltpu.VMEM(w.shape, BF16) for w in weights]
    return pl.pallas_call(
        functools.partial(_mixer_kernel, carry, tile, subtiles),
        out_shape=out_shape,
        grid=grid,
        in_specs=in_specs,
        out_specs=out_specs,
        scratch_shapes=scratch,
        compiler_params=pltpu.CompilerParams(
            dimension_semantics=("arbitrary",), vmem_limit_bytes=VMEM_LIMIT_BYTES),
        name="mixer_prompt" if carry else "mixer_sample",
    )(*inputs)


def _mlp_kernel(tile, subtiles, h_ref, n2_ref, w1_ref, w2_ref, nf_ref, y_ref):
    for sub in range(subtiles):
        rows = slice(sub * tile, (sub + 1) * tile)
        h = h_ref[rows, :]
        hn = _rms(h, n2_ref[0:1, :]).astype(BF16)
        a = jnp.maximum(_dot(hn, w1_ref[...]), 0.0)
        a2 = (a * a).astype(BF16)
        half = tile // 2
        for r in range(2):
            rs = slice(r * half, (r + 1) * half)
            o = h[rs] + _dot(a2[rs], w2_ref[...])
            y_ref[sub * tile + r * half:sub * tile + (r + 1) * half, :] = _rms(o, nf_ref[0:1, :])


def _mlp(h, n2, w1, w2, nf, name):
    n, d = h.shape
    tile, subtiles = MLP_TILE, MLP_SUBTILES
    tok = pl.BlockSpec((tile * subtiles, d), lambda i: (i, 0))
    weights = [n2, w1, w2, nf]
    return pl.pallas_call(
        functools.partial(_mlp_kernel, tile, subtiles),
        out_shape=jax.ShapeDtypeStruct((n, d), F32),
        grid=(n // (tile * subtiles),),
        in_specs=[tok] + [_resident(w.shape) for w in weights],
        out_specs=tok,
        compiler_params=pltpu.CompilerParams(
            dimension_semantics=("arbitrary",), vmem_limit_bytes=VMEM_LIMIT_BYTES),
        name=name,
    )(h, *weights)


def kernel(x_prompt, x_sample, state_ret, state_conv, norm1, w_in, ret_gn_w, conv_w, w_ret_out, w_conv_out, w_o, norm2, w_ff1, w_ff2, norm_f):
    depth = w_in.shape[0]
    assert depth == 1, "single-layer kernel"
    bp, lp, d = x_prompt.shape
    bs, ls, _ = x_sample.shape
    assert bp == 1 and ls == CHUNK
    assert lp % (MIXER_TILE * PROMPT_SUBTILES) == 0 and (bs * ls) % (MIXER_TILE * SAMPLE_SUBTILES) == 0
    assert lp % (MLP_TILE * MLP_SUBTILES) == 0 and (bs * ls) % (MLP_TILE * MLP_SUBTILES) == 0

    row = lambda v: jnp.pad(v.reshape(1, -1), ((0, F32_SUBLANES - 1), (0, 0)))
    small = (row(norm1[0]), row(ret_gn_w[0]), conv_w[0])

    hp, ret_p, conv_p, *bf = _mixer(
        x_prompt.reshape(lp, d), None, None, *small,
        (w_in[0], w_ret_out[0], w_conv_out[0], w_o[0]), ff=(w_ff1[0], w_ff2[0]), carry=True)
    mixer_bf, (ff1_bf, ff2_bf) = bf[:4], bf[4:]
    mlp_w = (row(norm2[0]), ff1_bf, ff2_bf, row(norm_f))
    conv_hist = jnp.pad(state_conv[0], ((0, 0), (HIST_ROWS - (CONV_W - 1), 0), (0, 0)))
    hs, ret_s, conv_s = _mixer(x_sample.reshape(bs * ls, d), state_ret[0], conv_hist, *small, mixer_bf, carry=False)

    yp = _mlp(hp, *mlp_w, name="mlp_prompt").reshape(bp, lp, d)
    ys = _mlp(hs, *mlp_w, name="mlp_sample").reshape(bs, ls, d)
    keep = slice(HIST_ROWS - (CONV_W - 1), HIST_ROWS)
    return (yp, ys, ret_p[None, None], conv_p[None, None, keep],
            ret_s[None], conv_s[None, :, keep])
```

```python
import functools
import math

import jax
import jax.numpy as jnp
from jax import lax
from jax.experimental import pallas as pl
from jax.experimental.pallas import tpu as pltpu

CHUNK = 64
CHUNK_SHIFT = 6
HEADS = 8
HEAD_DIM = 128
CONV_W = 3
ROPE_BASE = 10000.0
EPS = 1e-6
PAST_LEN = 4096
HIST_ROWS = 8
BF16_SUBLANES = 16

MIXER_TILE = 256
PROMPT_SUBTILES = 2
SAMPLE_SUBTILES = 1
MLP_TILE = 512
MLP_SUBTILES = 2
CONVERT_STEPS = 16
VMEM_LIMIT_BYTES = 60 * 1024 * 1024

F32 = jnp.float32
BF16 = jnp.bfloat16


def _log_gamma(h):
    return math.log(1.0 - 2.0 ** (-5.0 - h))


def _rms(x, w):
    ms = jnp.mean(x * x, axis=-1, keepdims=True)
    return (x * lax.rsqrt(ms + EPS)) * w


def _dot(a, b):
    return jnp.dot(a, b, preferred_element_type=F32)


def _dot_nt(a, b):
    return lax.dot_general(a, b, (((1,), (1,)), ((), ())), preferred_element_type=F32)


def _dot_tn(a, b):
    return lax.dot_general(a, b, (((0,), (0,)), ((), ())), preferred_element_type=F32)


def _rope_cos_sin(pos):
    lane = lax.broadcasted_iota(jnp.int32, pos.shape, 1)
    half = HEAD_DIM // 2
    inv = jnp.exp((-math.log(ROPE_BASE)) * (lane & (half - 1)).astype(F32) / half)
    ang = pos * inv
    return jnp.cos(ang), jnp.where(lane < half, -1.0, 1.0) * jnp.sin(ang)


def _mixer_kernel(carry, tile, subtiles, *refs):
    n_chunks = tile // CHUNK
    if carry:
        (x_ref, n1_ref, gnw_ref, cw_ref, winf_ref, wrof_ref, wcof_ref, wof_ref, ff1_ref, ff2_ref,
         h_ref, sret_ref, sconv_ref, winb_ref, wrob_ref, wcob_ref, wob_ref, ff1_bf_ref, ff2_bf_ref,
         dmat, rowdec, coldec, cos_s, sin_s, xn_s, q_s, k_s, kd_s, v_s, sg_s, gated_s, ubuf,
         cosb_s, sinb_s, win_ref, wro_ref, wco_ref, wo_ref) = refs
        sin_ret_ref = sin_conv_ref = None
    else:
        (x_ref, sin_ret_ref, sin_conv_ref, n1_ref, gnw_ref, cw_ref, win_ref, wro_ref, wco_ref, wo_ref,
         h_ref, sret_ref, sconv_ref,
         dmat, rowdec, coldec, cos_s, sin_s, xn_s, q_s, k_s, kd_s, v_s, sg_s, gated_s, ubuf) = refs
    d_model = x_ref.shape[-1]
    step = pl.program_id(0)
    state_len = tile if carry else CHUNK

    @pl.when(step == 0)
    def _init():
        ii = lax.broadcasted_iota(jnp.int32, (tile, tile), 0)
        jj = lax.broadcasted_iota(jnp.int32, (tile, tile), 1)
        dist = jnp.abs(ii - jj).astype(F32)
        ci, cj = ii >> CHUNK_SHIFT, jj >> CHUNK_SHIFT
        keep = (cj <= ci) if carry else (cj == ci)
        row = lax.broadcasted_iota(jnp.int32, (tile, HEAD_DIM), 0)
        loc = (row if carry else row & (CHUNK - 1)).astype(F32)
        for h in range(HEADS):
            lg = _log_gamma(h)
            dmat[h] = jnp.where(keep, jnp.exp(lg * dist), 0.0)
            rowdec[h] = jnp.exp(lg * (loc + 1.0))
            coldec[h] = jnp.exp(lg * (state_len - 1.0 - loc))
        if carry:
            cosb_s[...], sinb_s[...] = _rope_cos_sin(row.astype(F32))
            sret_ref[...] = jnp.zeros_like(sret_ref)
            ubuf[0:HIST_ROWS, :] = jnp.zeros((HIST_ROWS, d_model), F32)
        else:
            cos, sin = _rope_cos_sin((PAST_LEN + (row & (CHUNK - 1))).astype(F32))
            for sub in range(subtiles):
                cos_s[sub], sin_s[sub] = cos, sin

    def tiles(t):
        w0, w1, w2 = cw_ref[0:1, :], cw_ref[1:2, :], cw_ref[2:3, :]

        for sub in range(subtiles):
            rows = slice(sub * tile, (sub + 1) * tile)
            x = x_ref[rows, :]
            xn_s[sub] = _rms(x, n1_ref[...]).astype(BF16)

            def proj(g):
                return _dot(xn_s[sub], win_ref[:, g * d_model:(g + 1) * d_model])

            if carry:
                first = jnp.zeros((8, HEAD_DIM), jnp.int32) + (t * subtiles + sub) * tile
                ct, st = _rope_cos_sin(first.astype(F32))
                ct, st = ct[0:1, :], st[0:1, :]
                cb, sb = cosb_s[...], sinb_s[...]
                cos_s[sub] = cb * ct - sb * st
                sin_s[sub] = sb * ct + cb * st

            def rope(z):
                return z * cos_s[sub] + pltpu.roll(z, HEAD_DIM // 2, 1) * sin_s[sub]

            q = proj(0)
            for h in range(HEADS):
                hs = slice(h * HEAD_DIM, (h + 1) * HEAD_DIM)
                q_s[sub, :, hs] = rope(q[:, hs]).astype(BF16)
            k = proj(1)
            for h in range(HEADS):
                hs = slice(h * HEAD_DIM, (h + 1) * HEAD_DIM)
                kr = rope(k[:, hs]) * (HEAD_DIM ** -0.5)
                k_s[sub, :, hs] = kr.astype(BF16)
                kd_s[sub, :, hs] = (kr * coldec[h]).astype(BF16)
            v_s[sub] = proj(2).astype(BF16)

            def scores(h):
                hs = slice(h * HEAD_DIM, (h + 1) * HEAD_DIM)
                return (_dot_nt(q_s[sub, :, hs], k_s[sub, :, hs]) * dmat[h]).astype(BF16)

            p_next = scores(0)
            sg_s[sub] = jax.nn.silu(proj(3))
            for h in range(HEADS):
                hs = slice(h * HEAD_DIM, (h + 1) * HEAD_DIM)
                decay = math.exp(_log_gamma(h) * state_len)
                qh, kdh, vh = q_s[sub, :, hs], kd_s[sub, :, hs], v_s[sub, :, hs]
                p = p_next
                if h + 1 < HEADS:
                    p_next = scores(h + 1)
                o = _dot(p, vh)
                if carry:
                    s0 = sret_ref[h]
                    o = o + _dot(qh, s0.astype(BF16)) * rowdec[h]
                    sret_ref[h] = s0 * decay + _dot_tn(kdh, vh)
                else:
                    inter = []
                    for c in range(n_chunks):
                        rs = slice(c * CHUNK, (c + 1) * CHUNK)
                        seq = sub * n_chunks + c
                        s0 = sin_ret_ref[seq, h]
                        inter.append(_dot(qh[rs], s0.astype(BF16)))
                        sret_ref[seq, h] = s0 * decay + _dot_tn(kdh[rs], vh[rs])
                    o = o + jnp.concatenate(inter, axis=0) * rowdec[h]
                mu = jnp.mean(o, axis=-1, keepdims=True)
                d = o - mu
                var = jnp.mean(d * d, axis=-1, keepdims=True)
                on = (d * lax.rsqrt(var + EPS)) * gnw_ref[:, hs]
                gated_s[sub, :, hs] = (on * sg_s[sub, :, hs]).astype(BF16)

            u = proj(5) * proj(6)
            ret_y = _dot(gated_s[sub], wro_ref[...])
            mix = jax.nn.sigmoid(proj(7)) * ret_y
            if carry:
                base = HIST_ROWS + sub * tile
                ubuf[base:base + tile, :] = u
                y = (w0 * ubuf[base - 2:base - 2 + tile, :]
                     + w1 * ubuf[base - 1:base - 1 + tile, :]) + w2 * u
            else:
                ys = []
                for c in range(n_chunks):
                    seq = sub * n_chunks + c
                    base = seq * (CHUNK + HIST_ROWS) + HIST_ROWS
                    ubuf[base - HIST_ROWS:base, :] = sin_conv_ref[seq]
                    ubuf[base:base + CHUNK, :] = u[c * CHUNK:(c + 1) * CHUNK]
                    ys.append((w0 * ubuf[base - 2:base - 2 + CHUNK, :]
                               + w1 * ubuf[base - 1:base - 1 + CHUNK, :])
                              + w2 * ubuf[base:base + CHUNK, :])
                    sconv_ref[seq] = ubuf[base + CHUNK - HIST_ROWS:base + CHUNK, :]
                y = jnp.concatenate(ys, axis=0)
            conv_y = _dot((proj(4) * y).astype(BF16), wco_ref[...])
            mix = mix + jax.nn.sigmoid(proj(8)) * conv_y

            h_ref[rows, :] = x + _dot(mix.astype(BF16), wo_ref[...])

        if carry:
            tail = ubuf[subtiles * tile:subtiles * tile + HIST_ROWS, :]
            sconv_ref[...] = tail
            ubuf[0:HIST_ROWS, :] = tail

    if carry:
        n_conv = win_ref.shape[0] // winf_ref.shape[0]

        @pl.when(step < n_conv)
        def _convert():
            for f_ref, w_ref in ((winf_ref, win_ref), (wrof_ref, wro_ref),
                                 (wcof_ref, wco_ref), (wof_ref, wo_ref)):
                slab = f_ref.shape[0]
                w_ref[pl.ds(pl.multiple_of(step * slab, slab), slab), :] = f_ref[...].astype(BF16)

        @pl.when(step >= n_conv)
        def _tiles():
            t = step - n_conv
            ff1_bf_ref[...] = ff1_ref[...].astype(BF16)
            ff2_bf_ref[...] = ff2_ref[...].astype(BF16)
            for b_ref, w_ref in ((winb_ref, win_ref), (wrob_ref, wro_ref),
                                 (wcob_ref, wco_ref), (wob_ref, wo_ref)):
                slab = b_ref.shape[0]
                b_ref[...] = w_ref[pl.ds(pl.multiple_of(t * slab, slab), slab), :]
            tiles(t)
    else:
        tiles(step)


def _resident(shape):
    nd = len(shape)
    return pl.BlockSpec(shape, lambda i: (0,) * nd, pipeline_mode=pl.Buffered(1))


def _mixer(x, state_ret, state_conv, n1, gnw, cw, weights, ff=None, *, carry):
    n, d = x.shape
    tile = MIXER_TILE
    subtiles = PROMPT_SUBTILES if carry else SAMPLE_SUBTILES
    rows = tile * subtiles
    n_seqs = rows // CHUNK
    tile_steps = n // rows
    n_conv = CONVERT_STEPS if carry else 0
    grid = (n_conv + tile_steps,)
    tile_idx = lambda i: jnp.maximum(i - n_conv, 0)
    tok = pl.BlockSpec((rows, d), lambda i: (tile_idx(i), 0))
    small = [n1, gnw, cw]
    small_specs = [_resident(w.shape) for w in small]

    def slabs(w, steps, index):
        slab = w.shape[0] // steps
        assert slab * steps == w.shape[0] and slab % BF16_SUBLANES == 0
        return pl.BlockSpec((slab, w.shape[1]), lambda i: (index(i), 0))

    if carry:
        conv_idx = lambda i: jnp.minimum(i, n_conv - 1)
        w_slabs = [slabs(w, n_conv, conv_idx) for w in weights]
        ff_slabs = [slabs(w, tile_steps, tile_idx) for w in ff]
        inputs = [x] + small + list(weights) + list(ff)
        in_specs = [tok] + small_specs + w_slabs + ff_slabs
        out_shape = (jax.ShapeDtypeStruct((n, d), F32),
                     jax.ShapeDtypeStruct((HEADS, HEAD_DIM, HEAD_DIM), F32),
                     jax.ShapeDtypeStruct((HIST_ROWS, d), F32),
                     *[jax.ShapeDtypeStruct(w.shape, BF16) for w in (*weights, *ff)])
        out_specs = (tok,
                     pl.BlockSpec((HEADS, HEAD_DIM, HEAD_DIM), lambda i: (0, 0, 0)),
                     pl.BlockSpec((HIST_ROWS, d), lambda i: (0, 0)),
                     *[slabs(w, tile_steps, tile_idx) for w in weights], *ff_slabs)
        conv_rows = rows + HIST_ROWS
    else:
        ret_spec = pl.BlockSpec((n_seqs, HEADS, HEAD_DIM, HEAD_DIM), lambda i: (i, 0, 0, 0))
        conv_spec = pl.BlockSpec((n_seqs, HIST_ROWS, d), lambda i: (i, 0, 0))
        inputs = [x, state_ret, state_conv] + small + list(weights)
        in_specs = [tok, ret_spec, conv_spec] + small_specs + [_resident(w.shape) for w in weights]
        out_shape = (jax.ShapeDtypeStruct((n, d), F32),
                     jax.ShapeDtypeStruct(state_ret.shape, F32),
                     jax.ShapeDtypeStruct(state_conv.shape, F32))
        out_specs = (tok, ret_spec, conv_spec)
        conv_rows = n_seqs * (CHUNK + HIST_ROWS)
    scratch = [
        pltpu.VMEM((HEADS, tile, tile), F32),
        pltpu.VMEM((HEADS, tile, HEAD_DIM), F32),
        pltpu.VMEM((HEADS, tile, HEAD_DIM), F32),
        pltpu.VMEM((subtiles, tile, HEAD_DIM), F32),
        pltpu.VMEM((subtiles, tile, HEAD_DIM), F32),
        pltpu.VMEM((subtiles, tile, d), BF16),
        pltpu.VMEM((subtiles, tile, d), BF16),
        pltpu.VMEM((subtiles, tile, d), BF16),
        pltpu.VMEM((subtiles, tile, d), BF16),
        pltpu.VMEM((subtiles, tile, d), BF16),
        pltpu.VMEM((subtiles, tile, d), F32),
        pltpu.VMEM((subtiles, tile, d), BF16),
        pltpu.VMEM((conv_rows, d), F32),
    ]
    if carry:
        scratch += [pltpu.VMEM((tile, HEAD_DIM), F32)] * 2
        scratch += [pltpu.VMEM(w.shape, BF16) for w in weights]
    return pl.pallas_call(
        functools.partial(_mixer_kernel, carry, tile, subtiles),
        out_shape=out_shape,
        grid=grid,
        in_specs=in_specs,
        out_specs=out_specs,
        scratch_shapes=scratch,
        compiler_params=pltpu.CompilerParams(
            dimension_semantics=("arbitrary",), vmem_limit_bytes=VMEM_LIMIT_BYTES),
        name="mixer_prompt" if carry else "mixer_sample",
    )(*inputs)


def _mlp_kernel(tile, subtiles, h_ref, n2_ref, w1_ref, w2_ref, nf_ref, y_ref):
    for sub in range(subtiles):
        rows = slice(sub * tile, (sub + 1) * tile)
        h = h_ref[rows, :]
        hn = _rms(h, n2_ref[...]).astype(BF16)
        a = jnp.maximum(_dot(hn, w1_ref[...]), 0.0)
        a2 = (a * a).astype(BF16)
        half = tile // 2
        for r in range(2):
            rs = slice(r * half, (r + 1) * half)
            o = h[rs] + _dot(a2[rs], w2_ref[...])
            y_ref[sub * tile + r * half:sub * tile + (r + 1) * half, :] = _rms(o, nf_ref[...])


def _mlp(h, n2, w1, w2, nf, name):
    n, d = h.shape
    tile, subtiles = MLP_TILE, MLP_SUBTILES
    tok = pl.BlockSpec((tile * subtiles, d), lambda i: (i, 0))
    weights = [n2, w1, w2, nf]
    return pl.pallas_call(
        functools.partial(_mlp_kernel, tile, subtiles),
        out_shape=jax.ShapeDtypeStruct((n, d), F32),
        grid=(n // (tile * subtiles),),
        in_specs=[tok] + [_resident(w.shape) for w in weights],
        out_specs=tok,
        compiler_params=pltpu.CompilerParams(
            dimension_semantics=("arbitrary",), vmem_limit_bytes=VMEM_LIMIT_BYTES),
        name=name,
    )(h, *weights)


def kernel(x_prompt, x_sample, state_ret, state_conv, norm1, w_in, ret_gn_w, conv_w, w_ret_out, w_conv_out, w_o, norm2, w_ff1, w_ff2, norm_f):
    depth = w_in.shape[0]
    assert depth == 1, "single-layer kernel"
    bp, lp, d = x_prompt.shape
    bs, ls, _ = x_sample.shape
    assert bp == 1 and ls == CHUNK
    assert lp % (MIXER_TILE * PROMPT_SUBTILES) == 0 and (bs * ls) % (MIXER_TILE * SAMPLE_SUBTILES) == 0
    assert lp % (MLP_TILE * MLP_SUBTILES) == 0 and (bs * ls) % (MLP_TILE * MLP_SUBTILES) == 0

    row = lambda v: v.reshape(1, -1)
    small = (row(norm1[0]), row(ret_gn_w[0]), conv_w[0])

    hp, ret_p, conv_p, *bf = _mixer(
        x_prompt.reshape(lp, d), None, None, *small,
        (w_in[0], w_ret_out[0], w_conv_out[0], w_o[0]), ff=(w_ff1[0], w_ff2[0]), carry=True)
    mixer_bf, (ff1_bf, ff2_bf) = bf[:4], bf[4:]
    mlp_w = (row(norm2[0]), ff1_bf, ff2_bf, row(norm_f))
    conv_hist = jnp.pad(state_conv[0], ((0, 0), (HIST_ROWS - (CONV_W - 1), 0), (0, 0)))
    hs, ret_s, conv_s = _mixer(x_sample.reshape(bs * ls, d), state_ret[0], conv_hist, *small, mixer_bf, carry=False)

    yp = _mlp(hp, *mlp_w, name="mlp_prompt").reshape(bp, lp, d)
    ys = _mlp(hs, *mlp_w, name="mlp_sample").reshape(bs, ls, d)
    keep = slice(HIST_ROWS - (CONV_W - 1), HIST_ROWS)
    return (yp, ys, ret_p[None, None], conv_p[None, None, keep],
            ret_s[None], conv_s[None, :, keep])
```

```python
import functools
import math

import jax
import jax.numpy as jnp
from jax import lax
from jax.experimental import pallas as pl
from jax.experimental.pallas import tpu as pltpu

CHUNK = 64
CHUNK_SHIFT = 6
HEADS = 8
HEAD_DIM = 128
CONV_W = 3
ROPE_BASE = 10000.0
EPS = 1e-6
PAST_LEN = 4096
HIST_ROWS = 8
BF16_SUBLANES = 16
MXU_WIDTH = 256

MIXER_TILE = 256
PROMPT_SUBTILES = 2
SAMPLE_SUBTILES = 1
MLP_TILE = 512
MLP_SUBTILES = 2
CONVERT_STEPS = 16
VMEM_LIMIT_BYTES = 60 * 1024 * 1024

F32 = jnp.float32
BF16 = jnp.bfloat16


def _log_gamma(h):
    return math.log(1.0 - 2.0 ** (-5.0 - h))


def _rms(x, w):
    ms = jnp.mean(x * x, axis=-1, keepdims=True)
    return (x * lax.rsqrt(ms + EPS)) * w


def _dot(a, b):
    return jnp.dot(a, b, preferred_element_type=F32)


def _dot_nt(a, b):
    return lax.dot_general(a, b, (((1,), (1,)), ((), ())), preferred_element_type=F32)


def _dot_tn(a, b):
    return lax.dot_general(a, b, (((0,), (0,)), ((), ())), preferred_element_type=F32)


def _rope_cos_sin(pos):
    lane = lax.broadcasted_iota(jnp.int32, pos.shape, 1)
    half = HEAD_DIM // 2
    inv = jnp.exp((-math.log(ROPE_BASE)) * (lane & (half - 1)).astype(F32) / half)
    ang = pos * inv
    return jnp.cos(ang), jnp.where(lane < half, -1.0, 1.0) * jnp.sin(ang)


def _mixer_kernel(carry, tile, subtiles, *refs):
    n_chunks = tile // CHUNK
    if carry:
        (x_ref, n1_ref, gnw_ref, cw_ref, winf_ref, wrof_ref, wcof_ref, wof_ref, ff1_ref, ff2_ref,
         h_ref, sret_ref, sconv_ref, winb_ref, wrob_ref, wcob_ref, wob_ref, ff1_bf_ref, ff2_bf_ref,
         dmat, rowdec, coldec, cos_s, sin_s, xn_s, q_s, k_s, kd_s, v_s, sg_s, gated_s, ubuf,
         cosb_s, sinb_s, win_ref, wro_ref, wco_ref, wo_ref) = refs
        sin_ret_ref = sin_conv_ref = None
    else:
        (x_ref, sin_ret_ref, sin_conv_ref, n1_ref, gnw_ref, cw_ref, win_ref, wro_ref, wco_ref, wo_ref,
         h_ref, sret_ref, sconv_ref,
         dmat, rowdec, coldec, cos_s, sin_s, xn_s, q_s, k_s, kd_s, v_s, sg_s, gated_s, ubuf) = refs
    d_model = x_ref.shape[-1]
    step = pl.program_id(0)
    state_len = tile if carry else CHUNK

    @pl.when(step == 0)
    def _init():
        ii = lax.broadcasted_iota(jnp.int32, (tile, tile), 0)
        jj = lax.broadcasted_iota(jnp.int32, (tile, tile), 1)
        dist = jnp.abs(ii - jj).astype(F32)
        ci, cj = ii >> CHUNK_SHIFT, jj >> CHUNK_SHIFT
        keep = (cj <= ci) if carry else (cj == ci)
        row = lax.broadcasted_iota(jnp.int32, (tile, HEAD_DIM), 0)
        loc = (row if carry else row & (CHUNK - 1)).astype(F32)
        for h in range(HEADS):
            lg = _log_gamma(h)
            dmat[h] = jnp.where(keep, jnp.exp(lg * dist), 0.0)
            rowdec[h] = jnp.exp(lg * (loc + 1.0))
            coldec[h] = jnp.exp(lg * (state_len - 1.0 - loc))
        if carry:
            cosb_s[...], sinb_s[...] = _rope_cos_sin(row.astype(F32))
            sret_ref[...] = jnp.zeros_like(sret_ref)
            ubuf[0:HIST_ROWS, :] = jnp.zeros((HIST_ROWS, d_model), F32)
        else:
            cos, sin = _rope_cos_sin((PAST_LEN + (row & (CHUNK - 1))).astype(F32))
            for sub in range(subtiles):
                cos_s[sub], sin_s[sub] = cos, sin

    def tiles(t):
        w0, w1, w2 = cw_ref[0:1, :], cw_ref[1:2, :], cw_ref[2:3, :]

        for sub in range(subtiles):
            rows = slice(sub * tile, (sub + 1) * tile)
            x = x_ref[rows, :]
            xn = _rms(x, n1_ref[...]).astype(BF16)

            def proj(g, cs=None):
                lo, hi = (0, d_model) if cs is None else (cs.start, cs.stop)
                return _dot(xn, win_ref[:, g * d_model + lo:g * d_model + hi])

            slabs = [slice(c, c + MXU_WIDTH) for c in range(0, d_model, MXU_WIDTH)]

            if carry:
                first = jnp.zeros((8, HEAD_DIM), jnp.int32) + (t * subtiles + sub) * tile
                ct, st = _rope_cos_sin(first.astype(F32))
                ct, st = ct[0:1, :], st[0:1, :]
                cb, sb = cosb_s[...], sinb_s[...]
                cos_s[sub] = cb * ct - sb * st
                sin_s[sub] = sb * ct + cb * st

            def rope(z):
                return z * cos_s[sub] + pltpu.roll(z, HEAD_DIM // 2, 1) * sin_s[sub]

            heads = [slice(h * HEAD_DIM, (h + 1) * HEAD_DIM) for h in range(HEADS)]
            q = proj(0)
            qb = [rope(q[:, hs]).astype(BF16) for hs in heads]
            k = proj(1)
            kr = [rope(k[:, hs]) * (HEAD_DIM ** -0.5) for hs in heads]
            kb = [z.astype(BF16) for z in kr]
            kdb = [(z * coldec[h]).astype(BF16) for h, z in enumerate(kr)]
            vb = proj(2).astype(BF16)

            def scores(h):
                return (_dot_nt(qb[h], kb[h]) * dmat[h]).astype(BF16)

            p_next = scores(0)
            sg = jax.nn.silu(proj(3))
            gated = []
            for h in range(HEADS):
                hs = slice(h * HEAD_DIM, (h + 1) * HEAD_DIM)
                decay = math.exp(_log_gamma(h) * state_len)
                qh, kdh, vh = qb[h], kdb[h], vb[:, hs]
                p = p_next
                if h + 1 < HEADS:
                    p_next = scores(h + 1)
                o = _dot(p, vh)
                if carry:
                    s0 = sret_ref[h]
                    o = o + _dot(qh, s0.astype(BF16)) * rowdec[h]
                    sret_ref[h] = s0 * decay + _dot_tn(kdh, vh)
                else:
                    inter = []
                    for c in range(n_chunks):
                        rs = slice(c * CHUNK, (c + 1) * CHUNK)
                        seq = sub * n_chunks + c
                        s0 = sin_ret_ref[seq, h]
                        inter.append(_dot(qh[rs], s0.astype(BF16)))
                        sret_ref[seq, h] = s0 * decay + _dot_tn(kdh[rs], vh[rs])
                    o = o + jnp.concatenate(inter, axis=0) * rowdec[h]
                mu = jnp.mean(o, axis=-1, keepdims=True)
                d = o - mu
                var = jnp.mean(d * d, axis=-1, keepdims=True)
                on = (d * lax.rsqrt(var + EPS)) * gnw_ref[:, hs]
                gated.append((on * sg[:, hs]).astype(BF16))
            gated = jnp.concatenate(gated, axis=1)

            u = jnp.concatenate([proj(5, cs) * proj(6, cs) for cs in slabs], axis=1)
            mix = jnp.concatenate([jax.nn.sigmoid(proj(7, cs)) * _dot(gated, wro_ref[:, cs])
                                   for cs in slabs], axis=1)
            def taps(base, n, cs):
                return ((w0[:, cs] * ubuf[base - 2:base - 2 + n, cs]
                         + w1[:, cs] * ubuf[base - 1:base - 1 + n, cs]) + w2[:, cs] * ubuf[base:base + n, cs])

            if carry:
                base = HIST_ROWS + sub * tile
                ubuf[base:base + tile, :] = u
                conv = lambda cs: taps(base, tile, cs)
            else:
                bases = []
                for c in range(n_chunks):
                    seq = sub * n_chunks + c
                    cbase = seq * (CHUNK + HIST_ROWS) + HIST_ROWS
                    ubuf[cbase - HIST_ROWS:cbase, :] = sin_conv_ref[seq]
                    ubuf[cbase:cbase + CHUNK, :] = u[c * CHUNK:(c + 1) * CHUNK]
                    sconv_ref[seq] = ubuf[cbase + CHUNK - HIST_ROWS:cbase + CHUNK, :]
                    bases.append(cbase)
                conv = lambda cs: jnp.concatenate([taps(b, CHUNK, cs) for b in bases], axis=0)
            conv_in = jnp.concatenate([(proj(4, cs) * conv(cs)).astype(BF16) for cs in slabs], axis=1)
            mix = jnp.concatenate([mix[:, cs] + jax.nn.sigmoid(proj(8, cs)) * _dot(conv_in, wco_ref[:, cs])
                                   for cs in slabs], axis=1)

            h_ref[rows, :] = x_ref[rows, :] + _dot(mix.astype(BF16), wo_ref[...])

        if carry:
            tail = ubuf[subtiles * tile:subtiles * tile + HIST_ROWS, :]
            sconv_ref[...] = tail
            ubuf[0:HIST_ROWS, :] = tail

    if carry:
        n_conv = win_ref.shape[0] // winf_ref.shape[0]

        @pl.when(step < n_conv)
        def _convert():
            for f_ref, w_ref in ((winf_ref, win_ref), (wrof_ref, wro_ref),
                                 (wcof_ref, wco_ref), (wof_ref, wo_ref)):
                slab = f_ref.shape[0]
                w_ref[pl.ds(pl.multiple_of(step * slab, slab), slab), :] = f_ref[...].astype(BF16)

        @pl.when(step >= n_conv)
        def _tiles():
            t = step - n_conv
            ff1_bf_ref[...] = ff1_ref[...].astype(BF16)
            ff2_bf_ref[...] = ff2_ref[...].astype(BF16)
            for b_ref, w_ref in ((winb_ref, win_ref), (wrob_ref, wro_ref),
                                 (wcob_ref, wco_ref), (wob_ref, wo_ref)):
                slab = b_ref.shape[0]
                b_ref[...] = w_ref[pl.ds(pl.multiple_of(t * slab, slab), slab), :]
            tiles(t)
    else:
        tiles(step)


def _resident(shape):
    nd = len(shape)
    return pl.BlockSpec(shape, lambda i: (0,) * nd, pipeline_mode=pl.Buffered(1))


def _mixer(x, state_ret, state_conv, n1, gnw, cw, weights, ff=None, *, carry):
    n, d = x.shape
    tile = MIXER_TILE
    subtiles = PROMPT_SUBTILES if carry else SAMPLE_SUBTILES
    rows = tile * subtiles
    n_seqs = rows // CHUNK
    tile_steps = n // rows
    n_conv = CONVERT_STEPS if carry else 0
    grid = (n_conv + tile_steps,)
    tile_idx = lambda i: jnp.maximum(i - n_conv, 0)
    tok = pl.BlockSpec((rows, d), lambda i: (tile_idx(i), 0))
    small = [n1, gnw, cw]
    small_specs = [_resident(w.shape) for w in small]

    def slabs(w, steps, index):
        slab = w.shape[0] // steps
        assert slab * steps == w.shape[0] and slab % BF16_SUBLANES == 0
        return pl.BlockSpec((slab, w.shape[1]), lambda i: (index(i), 0))

    if carry:
        conv_idx = lambda i: jnp.minimum(i, n_conv - 1)
        w_slabs = [slabs(w, n_conv, conv_idx) for w in weights]
        ff_slabs = [slabs(w, tile_steps, tile_idx) for w in ff]
        inputs = [x] + small + list(weights) + list(ff)
        in_specs = [tok] + small_specs + w_slabs + ff_slabs
        out_shape = (jax.ShapeDtypeStruct((n, d), F32),
                     jax.ShapeDtypeStruct((HEADS, HEAD_DIM, HEAD_DIM), F32),
                     jax.ShapeDtypeStruct((HIST_ROWS, d), F32),
                     *[jax.ShapeDtypeStruct(w.shape, BF16) for w in (*weights, *ff)])
        out_specs = (tok,
                     pl.BlockSpec((HEADS, HEAD_DIM, HEAD_DIM), lambda i: (0, 0, 0)),
                     pl.BlockSpec((HIST_ROWS, d), lambda i: (0, 0)),
                     *[slabs(w, tile_steps, tile_idx) for w in weights], *ff_slabs)
        conv_rows = rows + HIST_ROWS
    else:
        ret_spec = pl.BlockSpec((n_seqs, HEADS, HEAD_DIM, HEAD_DIM), lambda i: (i, 0, 0, 0))
        conv_spec = pl.BlockSpec((n_seqs, HIST_ROWS, d), lambda i: (i, 0, 0))
        inputs = [x, state_ret, state_conv] + small + list(weights)
        in_specs = [tok, ret_spec, conv_spec] + small_specs + [_resident(w.shape) for w in weights]
        out_shape = (jax.ShapeDtypeStruct((n, d), F32),
                     jax.ShapeDtypeStruct(state_ret.shape, F32),
                     jax.ShapeDtypeStruct(state_conv.shape, F32))
        out_specs = (tok, ret_spec, conv_spec)
        conv_rows = n_seqs * (CHUNK + HIST_ROWS)
    scratch = [
        pltpu.VMEM((HEADS, tile, tile), F32),
        pltpu.VMEM((HEADS, tile, HEAD_DIM), F32),
        pltpu.VMEM((HEADS, tile, HEAD_DIM), F32),
        pltpu.VMEM((subtiles, tile, HEAD_DIM), F32),
        pltpu.VMEM((subtiles, tile, HEAD_DIM), F32),
        pltpu.VMEM((subtiles, tile, d), BF16),
        pltpu.VMEM((subtiles, tile, d), BF16),
        pltpu.VMEM((subtiles, tile, d), BF16),
        pltpu.VMEM((subtiles, tile, d), BF16),
        pltpu.VMEM((subtiles, tile, d), BF16),
        pltpu.VMEM((subtiles, tile, d), F32),
        pltpu.VMEM((subtiles, tile, d), BF16),
        pltpu.VMEM((conv_rows, d), F32),
    ]
    if carry:
        scratch += [pltpu.VMEM((tile, HEAD_DIM), F32)] * 2
        scratch += [pltpu.VMEM(w.shape, BF16) for w in weights]
    return pl.pallas_call(
        functools.partial(_mixer_kernel, carry, tile, subtiles),
        out_shape=out_shape,
        grid=grid,
        in_specs=in_specs,
        out_specs=out_specs,
        scratch_shapes=scratch,
        compiler_params=pltpu.CompilerParams(
            dimension_semantics=("arbitrary",), vmem_limit_bytes=VMEM_LIMIT_BYTES),
        name="mixer_prompt" if carry else "mixer_sample",
    )(*inputs)


def _mlp_kernel(tile, subtiles, h_ref, n2_ref, w1_ref, w2_ref, nf_ref, y_ref):
    for sub in range(subtiles):
        rows = slice(sub * tile, (sub + 1) * tile)
        h = h_ref[rows, :]
        hn = _rms(h, n2_ref[...]).astype(BF16)
        a = jnp.maximum(_dot(hn, w1_ref[...]), 0.0)
        a2 = (a * a).astype(BF16)
        half = tile // 2
        for r in range(2):
            rs = slice(r * half, (r + 1) * half)
            o = h[rs] + _dot(a2[rs], w2_ref[...])
            y_ref[sub * tile + r * half:sub * tile + (r + 1) * half, :] = _rms(o, nf_ref[...])


def _mlp(h, n2, w1, w2, nf, name):
    n, d = h.shape
    tile, subtiles = MLP_TILE, MLP_SUBTILES
    tok = pl.BlockSpec((tile * subtiles, d), lambda i: (i, 0))
    weights = [n2, w1, w2, nf]
    return pl.pallas_call(
        functools.partial(_mlp_kernel, tile, subtiles),
        out_shape=jax.ShapeDtypeStruct((n, d), F32),
        grid=(n // (tile * subtiles),),
        in_specs=[tok] + [_resident(w.shape) for w in weights],
        out_specs=tok,
        compiler_params=pltpu.CompilerParams(
            dimension_semantics=("arbitrary",), vmem_limit_bytes=VMEM_LIMIT_BYTES),
        name=name,
    )(h, *weights)


def kernel(x_prompt, x_sample, state_ret, state_conv, norm1, w_in, ret_gn_w, conv_w, w_ret_out, w_conv_out, w_o, norm2, w_ff1, w_ff2, norm_f):
    depth = w_in.shape[0]
    assert depth == 1, "single-layer kernel"
    bp, lp, d = x_prompt.shape
    bs, ls, _ = x_sample.shape
    assert bp == 1 and ls == CHUNK
    assert lp % (MIXER_TILE * PROMPT_SUBTILES) == 0 and (bs * ls) % (MIXER_TILE * SAMPLE_SUBTILES) == 0
    assert lp % (MLP_TILE * MLP_SUBTILES) == 0 and (bs * ls) % (MLP_TILE * MLP_SUBTILES) == 0

    row = lambda v: v.reshape(1, -1)
    small = (row(norm1[0]), row(ret_gn_w[0]), conv_w[0])

    hp, ret_p, conv_p, *bf = _mixer(
        x_prompt.reshape(lp, d), None, None, *small,
        (w_in[0], w_ret_out[0], w_conv_out[0], w_o[0]), ff=(w_ff1[0], w_ff2[0]), carry=True)
    mixer_bf, (ff1_bf, ff2_bf) = bf[:4], bf[4:]
    mlp_w = (row(norm2[0]), ff1_bf, ff2_bf, row(norm_f))
    conv_hist = jnp.pad(state_conv[0], ((0, 0), (HIST_ROWS - (CONV_W - 1), 0), (0, 0)))
    hs, ret_s, conv_s = _mixer(x_sample.reshape(bs * ls, d), state_ret[0], conv_hist, *small, mixer_bf, carry=False)

    yp = _mlp(hp, *mlp_w, name="mlp_prompt").reshape(bp, lp, d)
    ys = _mlp(hs, *mlp_w, name="mlp_sample").reshape(bs, ls, d)
    keep = slice(HIST_ROWS - (CONV_W - 1), HIST_ROWS)
    return (yp, ys, ret_p[None, None], conv_p[None, None, keep],
            ret_s[None], conv_s[None, :, keep])
```

```python
import functools
import math

import jax
import jax.numpy as jnp
from jax import lax
from jax.experimental import pallas as pl
from jax.experimental.pallas import tpu as pltpu

CHUNK = 64
CHUNK_SHIFT = 6
HEADS = 8
HEAD_DIM = 128
CONV_W = 3
ROPE_BASE = 10000.0
EPS = 1e-6
PAST_LEN = 4096
HIST_ROWS = 8
BF16_SUBLANES = 16
MXU_WIDTH = 256

MIXER_TILE = 256
PROMPT_SUBTILES = 2
SAMPLE_SUBTILES = 1
MLP_TILE = 512
MLP_SUBTILES = 2
CONVERT_STEPS = 16
VMEM_LIMIT_BYTES = 60 * 1024 * 1024

F32 = jnp.float32
BF16 = jnp.bfloat16


def _log_gamma(h):
    return math.log(1.0 - 2.0 ** (-5.0 - h))


def _rms(x, w):
    ms = jnp.mean(x * x, axis=-1, keepdims=True)
    return (x * lax.rsqrt(ms + EPS)) * w


def _dot(a, b):
    return jnp.dot(a, b, preferred_element_type=F32)


def _dot_nt(a, b):
    return lax.dot_general(a, b, (((1,), (1,)), ((), ())), preferred_element_type=F32)


def _dot_tn(a, b):
    return lax.dot_general(a, b, (((0,), (0,)), ((), ())), preferred_element_type=F32)


def _rope_cos_sin(pos):
    lane = lax.broadcasted_iota(jnp.int32, pos.shape, 1)
    half = HEAD_DIM // 2
    inv = jnp.exp((-math.log(ROPE_BASE)) * (lane & (half - 1)).astype(F32) / half)
    ang = pos * inv
    return jnp.cos(ang), jnp.where(lane < half, -1.0, 1.0) * jnp.sin(ang)


def _mixer_kernel(carry, tile, subtiles, *refs):
    n_chunks = tile // CHUNK
    if carry:
        (x_ref, n1_ref, gnw_ref, cw_ref, winf_ref, wrof_ref, wcof_ref, wof_ref, ff1_ref, ff2_ref,
         h_ref, sret_ref, sconv_ref, winb_ref, wrob_ref, wcob_ref, wob_ref, ff1_bf_ref, ff2_bf_ref,
         dmat, rowdec, coldec, cos_s, sin_s, ubuf,
         cosb_s, sinb_s, win_ref, wro_ref, wco_ref, wo_ref) = refs
        sin_ret_ref = sin_conv_ref = None
    else:
        (x_ref, sin_ret_ref, sin_conv_ref, n1_ref, gnw_ref, cw_ref, win_ref, wro_ref, wco_ref, wo_ref,
         h_ref, sret_ref, sconv_ref,
         dmat, rowdec, coldec, cos_s, sin_s, ubuf) = refs
    d_model = x_ref.shape[-1]
    step = pl.program_id(0)
    state_len = tile if carry else CHUNK

    @pl.when(step == 0)
    def _init():
        ii = lax.broadcasted_iota(jnp.int32, (tile, tile), 0)
        jj = lax.broadcasted_iota(jnp.int32, (tile, tile), 1)
        dist = jnp.abs(ii - jj).astype(F32)
        ci, cj = ii >> CHUNK_SHIFT, jj >> CHUNK_SHIFT
        keep = (cj <= ci) if carry else (cj == ci)
        row = lax.broadcasted_iota(jnp.int32, (tile, HEAD_DIM), 0)
        loc = (row if carry else row & (CHUNK - 1)).astype(F32)
        for h in range(HEADS):
            lg = _log_gamma(h)
            dmat[h] = jnp.where(keep, jnp.exp(lg * dist), 0.0)
            rowdec[h] = jnp.exp(lg * (loc + 1.0))
            coldec[h] = jnp.exp(lg * (state_len - 1.0 - loc))
        if carry:
            cosb_s[...], sinb_s[...] = _rope_cos_sin(row.astype(F32))
            sret_ref[...] = jnp.zeros_like(sret_ref)
            ubuf[0:HIST_ROWS, :] = jnp.zeros((HIST_ROWS, d_model), F32)
        else:
            cos, sin = _rope_cos_sin((PAST_LEN + (row & (CHUNK - 1))).astype(F32))
            for sub in range(subtiles):
                cos_s[sub], sin_s[sub] = cos, sin

    def tiles(t):
        w0, w1, w2 = cw_ref[0:1, :], cw_ref[1:2, :], cw_ref[2:3, :]

        for sub in range(subtiles):
            rows = slice(sub * tile, (sub + 1) * tile)
            x = x_ref[rows, :]
            xn = _rms(x, n1_ref[...]).astype(BF16)

            def proj(g, cs=None):
                lo, hi = (0, d_model) if cs is None else (cs.start, cs.stop)
                return _dot(xn, win_ref[:, g * d_model + lo:g * d_model + hi])

            slabs = [slice(c, c + MXU_WIDTH) for c in range(0, d_model, MXU_WIDTH)]

            if carry:
                first = jnp.zeros((8, HEAD_DIM), jnp.int32) + (t * subtiles + sub) * tile
                ct, st = _rope_cos_sin(first.astype(F32))
                ct, st = ct[0:1, :], st[0:1, :]
                cb, sb = cosb_s[...], sinb_s[...]
                cos_s[sub] = cb * ct - sb * st
                sin_s[sub] = sb * ct + cb * st

            def rope(z):
                return z * cos_s[sub] + pltpu.roll(z, HEAD_DIM // 2, 1) * sin_s[sub]

            def head_cols(g):
                for cs in slabs:
                    z = proj(g, cs)
                    for c in range(0, MXU_WIDTH, HEAD_DIM):
                        yield z[:, c:c + HEAD_DIM]

            qb = [rope(z).astype(BF16) for z in head_cols(0)]
            kb, kdb = [], []
            for h, z in enumerate(head_cols(1)):
                kr = rope(z) * (HEAD_DIM ** -0.5)
                kb.append(kr.astype(BF16))
                kdb.append((kr * coldec[h]).astype(BF16))
            vb = jnp.concatenate([proj(2, cs).astype(BF16) for cs in slabs], axis=1)

            def scores(h):
                return (_dot_nt(qb[h], kb[h]) * dmat[h]).astype(BF16)

            p_next = scores(0)
            sg = jnp.concatenate([jax.nn.silu(proj(3, cs)) for cs in slabs], axis=1)
            gated = []
            for h in range(HEADS):
                hs = slice(h * HEAD_DIM, (h + 1) * HEAD_DIM)
                decay = math.exp(_log_gamma(h) * state_len)
                qh, kdh, vh = qb[h], kdb[h], vb[:, hs]
                p = p_next
                if h + 1 < HEADS:
                    p_next = scores(h + 1)
                o = _dot(p, vh)
                if carry:
                    s0 = sret_ref[h]
                    o = o + _dot(qh, s0.astype(BF16)) * rowdec[h]
                    sret_ref[h] = s0 * decay + _dot_tn(kdh, vh)
                else:
                    inter = []
                    for c in range(n_chunks):
                        rs = slice(c * CHUNK, (c + 1) * CHUNK)
                        seq = sub * n_chunks + c
                        s0 = sin_ret_ref[seq, h]
                        inter.append(_dot(qh[rs], s0.astype(BF16)))
                        sret_ref[seq, h] = s0 * decay + _dot_tn(kdh[rs], vh[rs])
                    o = o + jnp.concatenate(inter, axis=0) * rowdec[h]
                mu = jnp.mean(o, axis=-1, keepdims=True)
                d = o - mu
                var = jnp.mean(d * d, axis=-1, keepdims=True)
                on = (d * lax.rsqrt(var + EPS)) * gnw_ref[:, hs]
                gated.append((on * sg[:, hs]).astype(BF16))
            gated = jnp.concatenate(gated, axis=1)

            u = jnp.concatenate([proj(5, cs) * proj(6, cs) for cs in slabs], axis=1)

            def taps(base, n, cs):
                return ((w0[:, cs] * ubuf[base - 2:base - 2 + n, cs]
                         + w1[:, cs] * ubuf[base - 1:base - 1 + n, cs]) + w2[:, cs] * ubuf[base:base + n, cs])

            if carry:
                base = HIST_ROWS + sub * tile
                ubuf[base:base + tile, :] = u
                conv = lambda cs: taps(base, tile, cs)
            else:
                bases = []
                for c in range(n_chunks):
                    seq = sub * n_chunks + c
                    cbase = seq * (CHUNK + HIST_ROWS) + HIST_ROWS
                    ubuf[cbase - HIST_ROWS:cbase, :] = sin_conv_ref[seq]
                    ubuf[cbase:cbase + CHUNK, :] = u[c * CHUNK:(c + 1) * CHUNK]
                    sconv_ref[seq] = ubuf[cbase + CHUNK - HIST_ROWS:cbase + CHUNK, :]
                    bases.append(cbase)
                conv = lambda cs: jnp.concatenate([taps(b, CHUNK, cs) for b in bases], axis=0)
            conv_in = jnp.concatenate([(proj(4, cs) * conv(cs)).astype(BF16) for cs in slabs], axis=1)

            mix = jnp.concatenate(
                [(jax.nn.sigmoid(proj(7, cs)) * _dot(gated, wro_ref[:, cs])
                  + jax.nn.sigmoid(proj(8, cs)) * _dot(conv_in, wco_ref[:, cs])).astype(BF16)
                 for cs in slabs], axis=1)
            h_ref[rows, :] = x_ref[rows, :] + _dot(mix, wo_ref[...])

        if carry:
            tail = ubuf[subtiles * tile:subtiles * tile + HIST_ROWS, :]
            sconv_ref[...] = tail
            ubuf[0:HIST_ROWS, :] = tail

    if carry:
        n_conv = win_ref.shape[0] // winf_ref.shape[0]

        @pl.when(step < n_conv)
        def _convert():
            for f_ref, w_ref in ((winf_ref, win_ref), (wrof_ref, wro_ref),
                                 (wcof_ref, wco_ref), (wof_ref, wo_ref)):
                slab = f_ref.shape[0]
                w_ref[pl.ds(pl.multiple_of(step * slab, slab), slab), :] = f_ref[...].astype(BF16)

        @pl.when(step >= n_conv)
        def _tiles():
            t = step - n_conv
            ff1_bf_ref[...] = ff1_ref[...].astype(BF16)
            ff2_bf_ref[...] = ff2_ref[...].astype(BF16)
            for b_ref, w_ref in ((winb_ref, win_ref), (wrob_ref, wro_ref),
                                 (wcob_ref, wco_ref), (wob_ref, wo_ref)):
                slab = b_ref.shape[0]
                b_ref[...] = w_ref[pl.ds(pl.multiple_of(t * slab, slab), slab), :]
            tiles(t)
    else:
        tiles(step)


def _resident(shape):
    nd = len(shape)
    return pl.BlockSpec(shape, lambda i: (0,) * nd, pipeline_mode=pl.Buffered(1))


def _mixer(x, state_ret, state_conv, n1, gnw, cw, weights, ff=None, *, carry):
    n, d = x.shape
    tile = MIXER_TILE
    subtiles = PROMPT_SUBTILES if carry else SAMPLE_SUBTILES
    rows = tile * subtiles
    n_seqs = rows // CHUNK
    tile_steps = n // rows
    n_conv = CONVERT_STEPS if carry else 0
    grid = (n_conv + tile_steps,)
    tile_idx = lambda i: jnp.maximum(i - n_conv, 0)
    tok = pl.BlockSpec((rows, d), lambda i: (tile_idx(i), 0))
    small = [n1, gnw, cw]
    small_specs = [_resident(w.shape) for w in small]

    def slabs(w, steps, index):
        slab = w.shape[0] // steps
        assert slab * steps == w.shape[0] and slab % BF16_SUBLANES == 0
        return pl.BlockSpec((slab, w.shape[1]), lambda i: (index(i), 0))

    if carry:
        conv_idx = lambda i: jnp.minimum(i, n_conv - 1)
        w_slabs = [slabs(w, n_conv, conv_idx) for w in weights]
        ff_slabs = [slabs(w, tile_steps, tile_idx) for w in ff]
        inputs = [x] + small + list(weights) + list(ff)
        in_specs = [tok] + small_specs + w_slabs + ff_slabs
        out_shape = (jax.ShapeDtypeStruct((n, d), F32),
                     jax.ShapeDtypeStruct((HEADS, HEAD_DIM, HEAD_DIM), F32),
                     jax.ShapeDtypeStruct((HIST_ROWS, d), F32),
                     *[jax.ShapeDtypeStruct(w.shape, BF16) for w in (*weights, *ff)])
        out_specs = (tok,
                     pl.BlockSpec((HEADS, HEAD_DIM, HEAD_DIM), lambda i: (0, 0, 0)),
                     pl.BlockSpec((HIST_ROWS, d), lambda i: (0, 0)),
                     *[slabs(w, tile_steps, tile_idx) for w in weights], *ff_slabs)
        conv_rows = rows + HIST_ROWS
    else:
        ret_spec = pl.BlockSpec((n_seqs, HEADS, HEAD_DIM, HEAD_DIM), lambda i: (i, 0, 0, 0))
        conv_spec = pl.BlockSpec((n_seqs, HIST_ROWS, d), lambda i: (i, 0, 0))
        inputs = [x, state_ret, state_conv] + small + list(weights)
        in_specs = [tok, ret_spec, conv_spec] + small_specs + [_resident(w.shape) for w in weights]
        out_shape = (jax.ShapeDtypeStruct((n, d), F32),
                     jax.ShapeDtypeStruct(state_ret.shape, F32),
                     jax.ShapeDtypeStruct(state_conv.shape, F32))
        out_specs = (tok, ret_spec, conv_spec)
        conv_rows = n_seqs * (CHUNK + HIST_ROWS)
    scratch = [
        pltpu.VMEM((HEADS, tile, tile), F32),
        pltpu.VMEM((HEADS, tile, HEAD_DIM), F32),
        pltpu.VMEM((HEADS, tile, HEAD_DIM), F32),
        pltpu.VMEM((subtiles, tile, HEAD_DIM), F32),
        pltpu.VMEM((subtiles, tile, HEAD_DIM), F32),
        pltpu.VMEM((conv_rows, d), F32),
    ]
    if carry:
        scratch += [pltpu.VMEM((tile, HEAD_DIM), F32)] * 2
        scratch += [pltpu.VMEM(w.shape, BF16) for w in weights]
    return pl.pallas_call(
        functools.partial(_mixer_kernel, carry, tile, subtiles),
        out_shape=out_shape,
        grid=grid,
        in_specs=in_specs,
        out_specs=out_specs,
        scratch_shapes=scratch,
        compiler_params=pltpu.CompilerParams(
            dimension_semantics=("arbitrary",), vmem_limit_bytes=VMEM_LIMIT_BYTES),
        name="mixer_prompt" if carry else "mixer_sample",
    )(*inputs)


def _mlp_kernel(tile, subtiles, h_ref, n2_ref, w1_ref, w2_ref, nf_ref, y_ref):
    for sub in range(subtiles):
        rows = slice(sub * tile, (sub + 1) * tile)
        h = h_ref[rows, :]
        hn = _rms(h, n2_ref[...]).astype(BF16)
        a = jnp.maximum(_dot(hn, w1_ref[...]), 0.0)
        a2 = (a * a).astype(BF16)
        half = tile // 2
        for r in range(2):
            rs = slice(r * half, (r + 1) * half)
            o = h[rs] + _dot(a2[rs], w2_ref[...])
            y_ref[sub * tile + r * half:sub * tile + (r + 1) * half, :] = _rms(o, nf_ref[...])


def _mlp(h, n2, w1, w2, nf, name):
    n, d = h.shape
    tile, subtiles = MLP_TILE, MLP_SUBTILES
    tok = pl.BlockSpec((tile * subtiles, d), lambda i: (i, 0))
    weights = [n2, w1, w2, nf]
    return pl.pallas_call(
        functools.partial(_mlp_kernel, tile, subtiles),
        out_shape=jax.ShapeDtypeStruct((n, d), F32),
        grid=(n // (tile * subtiles),),
        in_specs=[tok] + [_resident(w.shape) for w in weights],
        out_specs=tok,
        compiler_params=pltpu.CompilerParams(
            dimension_semantics=("arbitrary",), vmem_limit_bytes=VMEM_LIMIT_BYTES),
        name=name,
    )(h, *weights)


def kernel(x_prompt, x_sample, state_ret, state_conv, norm1, w_in, ret_gn_w, conv_w, w_ret_out, w_conv_out, w_o, norm2, w_ff1, w_ff2, norm_f):
    depth = w_in.shape[0]
    assert depth == 1, "single-layer kernel"
    bp, lp, d = x_prompt.shape
    bs, ls, _ = x_sample.shape
    assert bp == 1 and ls == CHUNK
    assert lp % (MIXER_TILE * PROMPT_SUBTILES) == 0 and (bs * ls) % (MIXER_TILE * SAMPLE_SUBTILES) == 0
    assert lp % (MLP_TILE * MLP_SUBTILES) == 0 and (bs * ls) % (MLP_TILE * MLP_SUBTILES) == 0

    row = lambda v: v.reshape(1, -1)
    small = (row(norm1[0]), row(ret_gn_w[0]), conv_w[0])

    hp, ret_p, conv_p, *bf = _mixer(
        x_prompt.reshape(lp, d), None, None, *small,
        (w_in[0], w_ret_out[0], w_conv_out[0], w_o[0]), ff=(w_ff1[0], w_ff2[0]), carry=True)
    mixer_bf, (ff1_bf, ff2_bf) = bf[:4], bf[4:]
    mlp_w = (row(norm2[0]), ff1_bf, ff2_bf, row(norm_f))
    conv_hist = jnp.pad(state_conv[0], ((0, 0), (HIST_ROWS - (CONV_W - 1), 0), (0, 0)))
    hs, ret_s, conv_s = _mixer(x_sample.reshape(bs * ls, d), state_ret[0], conv_hist, *small, mixer_bf, carry=False)

    yp = _mlp(hp, *mlp_w, name="mlp_prompt").reshape(bp, lp, d)
    ys = _mlp(hs, *mlp_w, name="mlp_sample").reshape(bs, ls, d)
    keep = slice(HIST_ROWS - (CONV_W - 1), HIST_ROWS)
    return (yp, ys, ret_p[None, None], conv_p[None, None, keep],
            ret_s[None], conv_s[None, :, keep])
```

```python
import functools
import math

import jax
import jax.numpy as jnp
from jax import lax
from jax.experimental import pallas as pl
from jax.experimental.pallas import tpu as pltpu

CHUNK = 64
CHUNK_SHIFT = 6
HEADS = 8
HEAD_DIM = 128
CONV_W = 3
ROPE_BASE = 10000.0
EPS = 1e-6
PAST_LEN = 4096
HIST_ROWS = 8
BF16_SUBLANES = 16
MXU_WIDTH = 256

MIXER_TILE = 256
PROMPT_SUBTILES = 2
SAMPLE_SUBTILES = 1
MLP_TILE = 512
MLP_SUBTILES = 2
CONVERT_STEPS = 16
VMEM_LIMIT_BYTES = 60 * 1024 * 1024

F32 = jnp.float32
BF16 = jnp.bfloat16


def _log_gamma(h):
    return math.log(1.0 - 2.0 ** (-5.0 - h))


def _rms(x, w):
    ms = jnp.mean(x * x, axis=-1, keepdims=True)
    return (x * lax.rsqrt(ms + EPS)) * w


def _dot(a, b):
    return jnp.dot(a, b, preferred_element_type=F32)


def _dot_nt(a, b):
    return lax.dot_general(a, b, (((1,), (1,)), ((), ())), preferred_element_type=F32)


def _dot_tn(a, b):
    return lax.dot_general(a, b, (((0,), (0,)), ((), ())), preferred_element_type=F32)


def _rope_cos_sin(pos):
    lane = lax.broadcasted_iota(jnp.int32, pos.shape, 1)
    half = HEAD_DIM // 2
    inv = jnp.exp((-math.log(ROPE_BASE)) * (lane & (half - 1)).astype(F32) / half)
    ang = pos * inv
    return jnp.cos(ang), jnp.sin(ang)


def _pair_layout(w):
    half = HEAD_DIM // 2
    blocks = [w[:, c:c + half] for c in range(0, w.shape[1], half)]
    order = [i + j for i in range(0, len(blocks), 4) for j in (0, 2, 1, 3)]
    return jnp.concatenate([blocks[i] for i in order], axis=1)


def _mixer_kernel(carry, tile, subtiles, *refs):
    n_chunks = tile // CHUNK
    if carry:
        (x_ref, n1_ref, gnw_ref, cw_ref, winf_ref, wrof_ref, wcof_ref, wof_ref, ff1_ref, ff2_ref,
         h_ref, sret_ref, sconv_ref, winb_ref, wrob_ref, wcob_ref, wob_ref, ff1_bf_ref, ff2_bf_ref,
         dmat, rowdec, coldec, cos_s, sin_s, ubuf,
         cosb_s, sinb_s, win_ref, wro_ref, wco_ref, wo_ref) = refs
        sin_ret_ref = sin_conv_ref = None
    else:
        (x_ref, sin_ret_ref, sin_conv_ref, n1_ref, gnw_ref, cw_ref, win_ref, wro_ref, wco_ref, wo_ref,
         h_ref, sret_ref, sconv_ref,
         dmat, rowdec, coldec, cos_s, sin_s, ubuf) = refs
    d_model = x_ref.shape[-1]
    step = pl.program_id(0)
    state_len = tile if carry else CHUNK

    @pl.when(step == 0)
    def _init():
        ii = lax.broadcasted_iota(jnp.int32, (tile, tile), 0)
        jj = lax.broadcasted_iota(jnp.int32, (tile, tile), 1)
        dist = jnp.abs(ii - jj).astype(F32)
        ci, cj = ii >> CHUNK_SHIFT, jj >> CHUNK_SHIFT
        keep = (cj <= ci) if carry else (cj == ci)
        row = lax.broadcasted_iota(jnp.int32, (tile, HEAD_DIM), 0)
        loc = (row if carry else row & (CHUNK - 1)).astype(F32)
        lane = lax.broadcasted_iota(jnp.int32, (tile, HEAD_DIM), 1)
        for h in range(HEADS):
            lg = _log_gamma(h)
            dmat[h] = jnp.where(keep, jnp.exp(lg * dist), 0.0)
            rowdec[h] = jnp.exp(lg * (loc + 1.0))
        for p in range(HEADS // 2):
            left = jnp.exp(_log_gamma(2 * p) * (state_len - 1.0 - loc))
            right = jnp.exp(_log_gamma(2 * p + 1) * (state_len - 1.0 - loc))
            coldec[p] = jnp.where(lane < HEAD_DIM // 2, left, right)
        if carry:
            cosb_s[...], sinb_s[...] = _rope_cos_sin(row.astype(F32))
            sret_ref[...] = jnp.zeros_like(sret_ref)
            ubuf[0:HIST_ROWS, :] = jnp.zeros((HIST_ROWS, d_model), F32)
        else:
            cos, sin = _rope_cos_sin((PAST_LEN + (row & (CHUNK - 1))).astype(F32))
            for sub in range(subtiles):
                cos_s[sub], sin_s[sub] = cos, sin

    def tiles(t):
        w0, w1, w2 = cw_ref[0:1, :], cw_ref[1:2, :], cw_ref[2:3, :]

        for sub in range(subtiles):
            rows = slice(sub * tile, (sub + 1) * tile)
            x = x_ref[rows, :]
            xn = _rms(x, n1_ref[...]).astype(BF16)

            def proj(g, cs=None):
                lo, hi = (0, d_model) if cs is None else (cs.start, cs.stop)
                return _dot(xn, win_ref[:, g * d_model + lo:g * d_model + hi])

            slabs = [slice(c, c + MXU_WIDTH) for c in range(0, d_model, MXU_WIDTH)]

            if carry:
                first = jnp.zeros((8, HEAD_DIM), jnp.int32) + (t * subtiles + sub) * tile
                ct, st = _rope_cos_sin(first.astype(F32))
                ct, st = ct[0:1, :], st[0:1, :]
                cb, sb = cosb_s[...], sinb_s[...]
                cos_s[sub] = cb * ct - sb * st
                sin_s[sub] = sb * ct + cb * st

            def rope(z):
                z1, z2 = z[:, :HEAD_DIM], z[:, HEAD_DIM:]
                c, s = cos_s[sub], sin_s[sub]
                return jnp.concatenate([z1 * c - z2 * s, z2 * c + z1 * s], axis=1)

            qp = [rope(proj(0, cs)).astype(BF16) for cs in slabs]
            kp, kdp = [], []
            for p, cs in enumerate(slabs):
                kr = rope(proj(1, cs)) * (HEAD_DIM ** -0.5)
                kp.append(kr.astype(BF16))
                kdp.append((kr * jnp.concatenate([coldec[p], coldec[p]], axis=1)).astype(BF16))
            vb = jnp.concatenate([proj(2, cs).astype(BF16) for cs in slabs], axis=1)
            first_of_pair = (lax.broadcasted_iota(jnp.int32, (tile, MXU_WIDTH), 1) & (HEAD_DIM // 2)) == 0

            def scores(h):
                q_h = jnp.where(first_of_pair if h % 2 == 0 else ~first_of_pair, qp[h // 2], 0)
                return (_dot_nt(q_h, kp[h // 2]) * dmat[h]).astype(BF16)

            def pair_state(sa, sb):
                z = jnp.zeros((HEAD_DIM // 2, HEAD_DIM), F32)
                m = HEAD_DIM // 2
                return jnp.concatenate([jnp.concatenate([sa[:m], z], axis=1), jnp.concatenate([z, sb[:m]], axis=1),
                                        jnp.concatenate([sa[m:], z], axis=1), jnp.concatenate([z, sb[m:]], axis=1)],
                                       axis=0).astype(BF16)

            def unpair(upd):
                m = HEAD_DIM // 2
                return (jnp.concatenate([upd[0:m, :HEAD_DIM], upd[2 * m:3 * m, :HEAD_DIM]], axis=0),
                        jnp.concatenate([upd[m:2 * m, HEAD_DIM:], upd[3 * m:, HEAD_DIM:]], axis=0))

            p_next = scores(0)
            sg = jnp.concatenate([jax.nn.silu(proj(3, cs)) for cs in slabs], axis=1)
            gated = []
            for h in range(HEADS):
                hs = slice(h * HEAD_DIM, (h + 1) * HEAD_DIM)
                vh = vb[:, hs]
                p = p_next
                if h + 1 < HEADS:
                    p_next = scores(h + 1)
                if h % 2 == 0:
                    pair = h // 2
                    v2 = vb[:, h * HEAD_DIM:(h + 2) * HEAD_DIM]
                    decays = [math.exp(_log_gamma(h + j) * state_len) for j in range(2)]
                    if carry:
                        s0 = [sret_ref[h], sret_ref[h + 1]]
                        q_state = _dot(qp[pair], pair_state(*s0))
                        for j, upd in enumerate(unpair(_dot_tn(kdp[pair], v2))):
                            sret_ref[h + j] = s0[j] * decays[j] + upd
                    else:
                        parts = []
                        for c in range(n_chunks):
                            rs = slice(c * CHUNK, (c + 1) * CHUNK)
                            seq = sub * n_chunks + c
                            s0 = [sin_ret_ref[seq, h], sin_ret_ref[seq, h + 1]]
                            parts.append(_dot(qp[pair][rs], pair_state(*s0)))
                            for j, upd in enumerate(unpair(_dot_tn(kdp[pair][rs], v2[rs]))):
                                sret_ref[seq, h + j] = s0[j] * decays[j] + upd
                        q_state = jnp.concatenate(parts, axis=0)
                lanes = slice((h % 2) * HEAD_DIM, (h % 2 + 1) * HEAD_DIM)
                o = _dot(p, vh) + q_state[:, lanes] * rowdec[h]
                mu = jnp.mean(o, axis=-1, keepdims=True)
                d = o - mu
                var = jnp.mean(d * d, axis=-1, keepdims=True)
                on = (d * lax.rsqrt(var + EPS)) * gnw_ref[:, hs]
                gated.append((on * sg[:, hs]).astype(BF16))
            gated = jnp.concatenate(gated, axis=1)

            u = jnp.concatenate([proj(5, cs) * proj(6, cs) for cs in slabs], axis=1)

            def taps(base, n, cs):
                return ((w0[:, cs] * ubuf[base - 2:base - 2 + n, cs]
                         + w1[:, cs] * ubuf[base - 1:base - 1 + n, cs]) + w2[:, cs] * ubuf[base:base + n, cs])

            if carry:
                base = HIST_ROWS + sub * tile
                ubuf[base:base + tile, :] = u
                conv = lambda cs: taps(base, tile, cs)
            else:
                bases = []
                for c in range(n_chunks):
                    seq = sub * n_chunks + c
                    cbase = seq * (CHUNK + HIST_ROWS) + HIST_ROWS
                    ubuf[cbase - HIST_ROWS:cbase, :] = sin_conv_ref[seq]
                    ubuf[cbase:cbase + CHUNK, :] = u[c * CHUNK:(c + 1) * CHUNK]
                    sconv_ref[seq] = ubuf[cbase + CHUNK - HIST_ROWS:cbase + CHUNK, :]
                    bases.append(cbase)
                conv = lambda cs: jnp.concatenate([taps(b, CHUNK, cs) for b in bases], axis=0)
            conv_in = jnp.concatenate([(proj(4, cs) * conv(cs)).astype(BF16) for cs in slabs], axis=1)

            mix = jnp.concatenate(
                [(jax.nn.sigmoid(proj(7, cs)) * _dot(gated, wro_ref[:, cs])
                  + jax.nn.sigmoid(proj(8, cs)) * _dot(conv_in, wco_ref[:, cs])).astype(BF16)
                 for cs in slabs], axis=1)
            h_ref[rows, :] = x_ref[rows, :] + _dot(mix, wo_ref[...])

        if carry:
            tail = ubuf[subtiles * tile:subtiles * tile + HIST_ROWS, :]
            sconv_ref[...] = tail
            ubuf[0:HIST_ROWS, :] = tail

    if carry:
        n_conv = win_ref.shape[0] // winf_ref.shape[0]

        @pl.when(step < n_conv)
        def _convert():
            for f_ref, w_ref in ((winf_ref, win_ref), (wrof_ref, wro_ref),
                                 (wcof_ref, wco_ref), (wof_ref, wo_ref)):
                slab = f_ref.shape[0]
                w = f_ref[...]
                if f_ref is winf_ref:
                    w = jnp.concatenate([_pair_layout(w[:, :2 * d_model]), w[:, 2 * d_model:]], axis=1)
                w_ref[pl.ds(pl.multiple_of(step * slab, slab), slab), :] = w.astype(BF16)

        @pl.when(step >= n_conv)
        def _tiles():
            t = step - n_conv
            ff1_bf_ref[...] = ff1_ref[...].astype(BF16)
            ff2_bf_ref[...] = ff2_ref[...].astype(BF16)
            for b_ref, w_ref in ((winb_ref, win_ref), (wrob_ref, wro_ref),
                                 (wcob_ref, wco_ref), (wob_ref, wo_ref)):
                slab = b_ref.shape[0]
                b_ref[...] = w_ref[pl.ds(pl.multiple_of(t * slab, slab), slab), :]
            tiles(t)
    else:
        tiles(step)


def _resident(shape):
    nd = len(shape)
    return pl.BlockSpec(shape, lambda i: (0,) * nd, pipeline_mode=pl.Buffered(1))


def _mixer(x, state_ret, state_conv, n1, gnw, cw, weights, ff=None, *, carry):
    n, d = x.shape
    tile = MIXER_TILE
    subtiles = PROMPT_SUBTILES if carry else SAMPLE_SUBTILES
    rows = tile * subtiles
    n_seqs = rows // CHUNK
    tile_steps = n // rows
    n_conv = CONVERT_STEPS if carry else 0
    grid = (n_conv + tile_steps,)
    tile_idx = lambda i: jnp.maximum(i - n_conv, 0)
    tok = pl.BlockSpec((rows, d), lambda i: (tile_idx(i), 0))
    small = [n1, gnw, cw]
    small_specs = [_resident(w.shape) for w in small]

    def slabs(w, steps, index):
        slab = w.shape[0] // steps
        assert slab * steps == w.shape[0] and slab % BF16_SUBLANES == 0
        return pl.BlockSpec((slab, w.shape[1]), lambda i: (index(i), 0))

    if carry:
        conv_idx = lambda i: jnp.minimum(i, n_conv - 1)
        w_slabs = [slabs(w, n_conv, conv_idx) for w in weights]
        ff_slabs = [slabs(w, tile_steps, tile_idx) for w in ff]
        inputs = [x] + small + list(weights) + list(ff)
        in_specs = [tok] + small_specs + w_slabs + ff_slabs
        out_shape = (jax.ShapeDtypeStruct((n, d), F32),
                     jax.ShapeDtypeStruct((HEADS, HEAD_DIM, HEAD_DIM), F32),
                     jax.ShapeDtypeStruct((HIST_ROWS, d), F32),
                     *[jax.ShapeDtypeStruct(w.shape, BF16) for w in (*weights, *ff)])
        out_specs = (tok,
                     pl.BlockSpec((HEADS, HEAD_DIM, HEAD_DIM), lambda i: (0, 0, 0)),
                     pl.BlockSpec((HIST_ROWS, d), lambda i: (0, 0)),
                     *[slabs(w, tile_steps, tile_idx) for w in weights], *ff_slabs)
        conv_rows = rows + HIST_ROWS
    else:
        ret_spec = pl.BlockSpec((n_seqs, HEADS, HEAD_DIM, HEAD_DIM), lambda i: (i, 0, 0, 0))
        conv_spec = pl.BlockSpec((n_seqs, HIST_ROWS, d), lambda i: (i, 0, 0))
        inputs = [x, state_ret, state_conv] + small + list(weights)
        in_specs = [tok, ret_spec, conv_spec] + small_specs + [_resident(w.shape) for w in weights]
        out_shape = (jax.ShapeDtypeStruct((n, d), F32),
                     jax.ShapeDtypeStruct(state_ret.shape, F32),
                     jax.ShapeDtypeStruct(state_conv.shape, F32))
        out_specs = (tok, ret_spec, conv_spec)
        conv_rows = n_seqs * (CHUNK + HIST_ROWS)
    scratch = [
        pltpu.VMEM((HEADS, tile, tile), F32),
        pltpu.VMEM((HEADS, tile, HEAD_DIM), F32),
        pltpu.VMEM((HEADS // 2, tile, HEAD_DIM), F32),
        pltpu.VMEM((subtiles, tile, HEAD_DIM), F32),
        pltpu.VMEM((subtiles, tile, HEAD_DIM), F32),
        pltpu.VMEM((conv_rows, d), F32),
    ]
    if carry:
        scratch += [pltpu.VMEM((tile, HEAD_DIM), F32)] * 2
        scratch += [pltpu.VMEM(w.shape, BF16) for w in weights]
    return pl.pallas_call(
        functools.partial(_mixer_kernel, carry, tile, subtiles),
        out_shape=out_shape,
        grid=grid,
        in_specs=in_specs,
        out_specs=out_specs,
        scratch_shapes=scratch,
        compiler_params=pltpu.CompilerParams(
            dimension_semantics=("arbitrary",), vmem_limit_bytes=VMEM_LIMIT_BYTES),
        name="mixer_prompt" if carry else "mixer_sample",
    )(*inputs)


def _mlp_kernel(tile, subtiles, h_ref, n2_ref, w1_ref, w2_ref, nf_ref, y_ref):
    for sub in range(subtiles):
        rows = slice(sub * tile, (sub + 1) * tile)
        h = h_ref[rows, :]
        hn = _rms(h, n2_ref[...]).astype(BF16)
        a = jnp.maximum(_dot(hn, w1_ref[...]), 0.0)
        a2 = (a * a).astype(BF16)
        half = tile // 2
        for r in range(2):
            rs = slice(r * half, (r + 1) * half)
            o = h[rs] + _dot(a2[rs], w2_ref[...])
            y_ref[sub * tile + r * half:sub * tile + (r + 1) * half, :] = _rms(o, nf_ref[...])


def _mlp(h, n2, w1, w2, nf, name):
    n, d = h.shape
    tile, subtiles = MLP_TILE, MLP_SUBTILES
    tok = pl.BlockSpec((tile * subtiles, d), lambda i: (i, 0))
    weights = [n2, w1, w2, nf]
    return pl.pallas_call(
        functools.partial(_mlp_kernel, tile, subtiles),
        out_shape=jax.ShapeDtypeStruct((n, d), F32),
        grid=(n // (tile * subtiles),),
        in_specs=[tok] + [_resident(w.shape) for w in weights],
        out_specs=tok,
        compiler_params=pltpu.CompilerParams(
            dimension_semantics=("arbitrary",), vmem_limit_bytes=VMEM_LIMIT_BYTES),
        name=name,
    )(h, *weights)


def kernel(x_prompt, x_sample, state_ret, state_conv, norm1, w_in, ret_gn_w, conv_w, w_ret_out, w_conv_out, w_o, norm2, w_ff1, w_ff2, norm_f):
    depth = w_in.shape[0]
    assert depth == 1, "single-layer kernel"
    bp, lp, d = x_prompt.shape
    bs, ls, _ = x_sample.shape
    assert bp == 1 and ls == CHUNK
    assert lp % (MIXER_TILE * PROMPT_SUBTILES) == 0 and (bs * ls) % (MIXER_TILE * SAMPLE_SUBTILES) == 0
    assert lp % (MLP_TILE * MLP_SUBTILES) == 0 and (bs * ls) % (MLP_TILE * MLP_SUBTILES) == 0

    row = lambda v: v.reshape(1, -1)
    small = (row(norm1[0]), row(ret_gn_w[0]), conv_w[0])

    hp, ret_p, conv_p, *bf = _mixer(
        x_prompt.reshape(lp, d), None, None, *small,
        (w_in[0], w_ret_out[0], w_conv_out[0], w_o[0]), ff=(w_ff1[0], w_ff2[0]), carry=True)
    mixer_bf, (ff1_bf, ff2_bf) = bf[:4], bf[4:]
    mlp_w = (row(norm2[0]), ff1_bf, ff2_bf, row(norm_f))
    conv_hist = jnp.pad(state_conv[0], ((0, 0), (HIST_ROWS - (CONV_W - 1), 0), (0, 0)))
    hs, ret_s, conv_s = _mixer(x_sample.reshape(bs * ls, d), state_ret[0], conv_hist, *small, mixer_bf, carry=False)

    yp = _mlp(hp, *mlp_w, name="mlp_prompt").reshape(bp, lp, d)
    ys = _mlp(hs, *mlp_w, name="mlp_sample").reshape(bs, ls, d)
    keep = slice(HIST_ROWS - (CONV_W - 1), HIST_ROWS)
    return (yp, ys, ret_p[None, None], conv_p[None, None, keep],
            ret_s[None], conv_s[None, :, keep])
```

```python
import functools
import math

import jax
import jax.numpy as jnp
from jax import lax
from jax.experimental import pallas as pl
from jax.experimental.pallas import tpu as pltpu

CHUNK = 64
CHUNK_SHIFT = CHUNK.bit_length() - 1
assert 1 << CHUNK_SHIFT == CHUNK
HEADS = 8
HEAD_DIM = 128
CONV_W = 3
ROPE_BASE = 10000.0
EPS = 1e-6
PAST_LEN = 4096

F32_SUBLANES = 8
BF16_SUBLANES = 16
MXU_WIDTH = 256
VMEM_BYTES = 64 * 1024 * 1024

HIST_ROWS = F32_SUBLANES

MIXER_TILE = 256
PROMPT_SUBTILES = 2
SAMPLE_SUBTILES = 1
MLP_TILE = 512
MLP_SUBTILES = 2
CONVERT_STEPS = 16
VMEM_LIMIT_BYTES = VMEM_BYTES - 4 * 1024 * 1024

F32 = jnp.float32
BF16 = jnp.bfloat16


def _log_gamma(h):
    return math.log(1.0 - 2.0 ** (-5.0 - h))


def _rms(x, w):
    ms = jnp.mean(x * x, axis=-1, keepdims=True)
    return (x * lax.rsqrt(ms + EPS)) * w


def _dot(a, b):
    return jnp.dot(a, b, preferred_element_type=F32)


def _dot_nt(a, b):
    return lax.dot_general(a, b, (((1,), (1,)), ((), ())), preferred_element_type=F32)


def _dot_tn(a, b):
    return lax.dot_general(a, b, (((0,), (0,)), ((), ())), preferred_element_type=F32)


def _rope_cos_sin(pos):
    lane = lax.broadcasted_iota(jnp.int32, pos.shape, 1)
    half = HEAD_DIM // 2
    inv = jnp.exp((-math.log(ROPE_BASE)) * (lane & (half - 1)).astype(F32) / half)
    ang = pos * inv
    return jnp.cos(ang), jnp.where(lane < half, -1.0, 1.0) * jnp.sin(ang)


def _mixer_kernel(carry, tile, subtiles, *refs):
    n_chunks = tile // CHUNK
    if carry:
        (x_ref, n1_ref, gnw_ref, cw_ref, winf_ref, wrof_ref, wcof_ref, wof_ref, ff1_ref, ff2_ref,
         h_ref, sret_ref, sconv_ref, winb_ref, wrob_ref, wcob_ref, wob_ref, ff1_bf_ref, ff2_bf_ref,
         dmat, rowdec, coldec, cos_s, sin_s, ubuf,
         cosb_s, sinb_s, win_ref, wro_ref, wco_ref, wo_ref) = refs
        sin_ret_ref = sin_conv_ref = None
    else:
        (x_ref, sin_ret_ref, sin_conv_ref, n1_ref, gnw_ref, cw_ref, win_ref, wro_ref, wco_ref, wo_ref,
         h_ref, sret_ref, sconv_ref,
         dmat, rowdec, coldec, cos_s, sin_s, ubuf) = refs
    d_model = x_ref.shape[-1]
    step = pl.program_id(0)
    state_len = tile if carry else CHUNK

    @pl.when(step == 0)
    def _init():
        ii = lax.broadcasted_iota(jnp.int32, (tile, tile), 0)
        jj = lax.broadcasted_iota(jnp.int32, (tile, tile), 1)
        dist = jnp.abs(ii - jj).astype(F32)
        ci, cj = ii >> CHUNK_SHIFT, jj >> CHUNK_SHIFT
        keep = (cj <= ci) if carry else (cj == ci)
        row = lax.broadcasted_iota(jnp.int32, (tile, HEAD_DIM), 0)
        loc = (row if carry else row & (CHUNK - 1)).astype(F32)
        for h in range(HEADS):
            lg = _log_gamma(h)
            dmat[h] = jnp.where(keep, jnp.exp(lg * dist), 0.0)
            rowdec[h] = jnp.exp(lg * (loc + 1.0))
            coldec[h] = jnp.exp(lg * (state_len - 1.0 - loc))
        if carry:
            cosb_s[...], sinb_s[...] = _rope_cos_sin(row.astype(F32))
            sret_ref[...] = jnp.zeros_like(sret_ref)
            ubuf[0:HIST_ROWS, :] = jnp.zeros((HIST_ROWS, d_model), F32)
        else:
            cos, sin = _rope_cos_sin((PAST_LEN + (row & (CHUNK - 1))).astype(F32))
            for sub in range(subtiles):
                cos_s[sub], sin_s[sub] = cos, sin

    def tiles(t):
        w0, w1, w2 = cw_ref[0:1, :], cw_ref[1:2, :], cw_ref[2:3, :]

        for sub in range(subtiles):
            rows = slice(sub * tile, (sub + 1) * tile)
            x = x_ref[rows, :]
            xn = _rms(x, n1_ref[...]).astype(BF16)

            def proj(g, cs=None):
                lo, hi = (0, d_model) if cs is None else (cs.start, cs.stop)
                return _dot(xn, win_ref[:, g * d_model + lo:g * d_model + hi])

            slabs = [slice(c, c + MXU_WIDTH) for c in range(0, d_model, MXU_WIDTH)]

            if carry:
                first = jnp.zeros((F32_SUBLANES, HEAD_DIM), jnp.int32) + (t * subtiles + sub) * tile
                ct, st = _rope_cos_sin(first.astype(F32))
                ct, st = ct[0:1, :], st[0:1, :]
                cb, sb = cosb_s[...], sinb_s[...]
                cos_s[sub] = cb * ct - sb * st
                sin_s[sub] = sb * ct + cb * st

            def rope(z):
                return z * cos_s[sub] + pltpu.roll(z, HEAD_DIM // 2, 1) * sin_s[sub]

            def head_cols(g):
                for cs in slabs:
                    z = proj(g, cs)
                    for c in range(0, MXU_WIDTH, HEAD_DIM):
                        yield z[:, c:c + HEAD_DIM]

            qb = [rope(z).astype(BF16) for z in head_cols(0)]
            kb, kdb = [], []
            for h, z in enumerate(head_cols(1)):
                kr = rope(z) * (HEAD_DIM ** -0.5)
                kb.append(kr.astype(BF16))
                kdb.append((kr * coldec[h]).astype(BF16))
            vb = jnp.concatenate([proj(2, cs).astype(BF16) for cs in slabs], axis=1)

            def scores(h):
                return (_dot_nt(qb[h], kb[h]) * dmat[h]).astype(BF16)

            p_next = scores(0)
            sg = jnp.concatenate([jax.nn.silu(proj(3, cs)) for cs in slabs], axis=1)
            gated = []
            for h in range(HEADS):
                hs = slice(h * HEAD_DIM, (h + 1) * HEAD_DIM)
                decay = math.exp(_log_gamma(h) * state_len)
                qh, kdh, vh = qb[h], kdb[h], vb[:, hs]
                p = p_next
                if h + 1 < HEADS:
                    p_next = scores(h + 1)
                o = _dot(p, vh)
                if carry:
                    s0 = sret_ref[h]
                    o = o + _dot(qh, s0.astype(BF16)) * rowdec[h]
                    sret_ref[h] = s0 * decay + _dot_tn(kdh, vh)
                else:
                    inter = []
                    for c in range(n_chunks):
                        rs = slice(c * CHUNK, (c + 1) * CHUNK)
                        seq = sub * n_chunks + c
                        s0 = sin_ret_ref[seq, h]
                        inter.append(_dot(qh[rs], s0.astype(BF16)))
                        sret_ref[seq, h] = s0 * decay + _dot_tn(kdh[rs], vh[rs])
                    o = o + jnp.concatenate(inter, axis=0) * rowdec[h]
                mu = jnp.mean(o, axis=-1, keepdims=True)
                d = o - mu
                var = jnp.mean(d * d, axis=-1, keepdims=True)
                on = (d * lax.rsqrt(var + EPS)) * gnw_ref[:, hs]
                gated.append((on * sg[:, hs]).astype(BF16))
            gated = jnp.concatenate(gated, axis=1)

            u = jnp.concatenate([proj(5, cs) * proj(6, cs) for cs in slabs], axis=1)

            def taps(base, n, cs):
                return ((w0[:, cs] * ubuf[base - 2:base - 2 + n, cs]
                         + w1[:, cs] * ubuf[base - 1:base - 1 + n, cs]) + w2[:, cs] * ubuf[base:base + n, cs])

            if carry:
                base = HIST_ROWS + sub * tile
                ubuf[base:base + tile, :] = u
                conv = lambda cs: taps(base, tile, cs)
            else:
                bases = []
                for c in range(n_chunks):
                    seq = sub * n_chunks + c
                    cbase = seq * (CHUNK + HIST_ROWS) + HIST_ROWS
                    ubuf[cbase - (CONV_W - 1):cbase, :] = sin_conv_ref[seq]
                    ubuf[cbase:cbase + CHUNK, :] = u[c * CHUNK:(c + 1) * CHUNK]
                    sconv_ref[seq] = ubuf[cbase + CHUNK - (CONV_W - 1):cbase + CHUNK, :]
                    bases.append(cbase)
                conv = lambda cs: jnp.concatenate([taps(b, CHUNK, cs) for b in bases], axis=0)
            conv_in = jnp.concatenate([(proj(4, cs) * conv(cs)).astype(BF16) for cs in slabs], axis=1)

            mix = jnp.concatenate(
                [(jax.nn.sigmoid(proj(7, cs)) * _dot(gated, wro_ref[:, cs])
                  + jax.nn.sigmoid(proj(8, cs)) * _dot(conv_in, wco_ref[:, cs])).astype(BF16)
                 for cs in slabs], axis=1)
            h_ref[rows, :] = x_ref[rows, :] + _dot(mix, wo_ref[...])

        if carry:
            end = subtiles * tile + HIST_ROWS
            sconv_ref[...] = ubuf[end - (CONV_W - 1):end, :]
            ubuf[0:HIST_ROWS, :] = ubuf[end - HIST_ROWS:end, :]

    if carry:
        n_conv = win_ref.shape[0] // winf_ref.shape[0]

        @pl.when(step < n_conv)
        def _convert():
            for f_ref, w_ref in ((winf_ref, win_ref), (wrof_ref, wro_ref),
                                 (wcof_ref, wco_ref), (wof_ref, wo_ref)):
                slab = f_ref.shape[0]
                w_ref[pl.ds(pl.multiple_of(step * slab, slab), slab), :] = f_ref[...].astype(BF16)

        @pl.when(step >= n_conv)
        def _tiles():
            t = step - n_conv
            ff1_bf_ref[...] = ff1_ref[...].astype(BF16)
            ff2_bf_ref[...] = ff2_ref[...].astype(BF16)
            for b_ref, w_ref in ((winb_ref, win_ref), (wrob_ref, wro_ref),
                                 (wcob_ref, wco_ref), (wob_ref, wo_ref)):
                slab = b_ref.shape[0]
                b_ref[...] = w_ref[pl.ds(pl.multiple_of(t * slab, slab), slab), :]
            tiles(t)
    else:
        tiles(step)


def _resident(shape):
    nd = len(shape)
    return pl.BlockSpec(shape, lambda i: (0,) * nd, pipeline_mode=pl.Buffered(1))


def _mixer(x, state_ret, state_conv, n1, gnw, cw, weights, ff=None, *, carry):
    n, d = x.shape
    tile = MIXER_TILE
    subtiles = PROMPT_SUBTILES if carry else SAMPLE_SUBTILES
    rows = tile * subtiles
    n_seqs = rows // CHUNK
    tile_steps = n // rows
    n_conv = CONVERT_STEPS if carry else 0
    grid = (n_conv + tile_steps,)
    tile_idx = lambda i: jnp.maximum(i - n_conv, 0)
    tok = pl.BlockSpec((rows, d), lambda i: (tile_idx(i), 0))
    small = [n1, gnw, cw]
    small_specs = [_resident(w.shape) for w in small]

    def slabs(w, steps, index):
        slab = w.shape[0] // steps
        assert slab * steps == w.shape[0] and slab % BF16_SUBLANES == 0
        return pl.BlockSpec((slab, w.shape[1]), lambda i: (index(i), 0))

    if carry:
        conv_idx = lambda i: jnp.minimum(i, n_conv - 1)
        w_slabs = [slabs(w, n_conv, conv_idx) for w in weights]
        ff_slabs = [slabs(w, tile_steps, tile_idx) for w in ff]
        inputs = [x] + small + list(weights) + list(ff)
        in_specs = [tok] + small_specs + w_slabs + ff_slabs
        out_shape = (jax.ShapeDtypeStruct((n, d), F32),
                     jax.ShapeDtypeStruct((HEADS, HEAD_DIM, HEAD_DIM), F32),
                     jax.ShapeDtypeStruct((CONV_W - 1, d), F32),
                     *[jax.ShapeDtypeStruct(w.shape, BF16) for w in (*weights, *ff)])
        out_specs = (tok,
                     pl.BlockSpec((HEADS, HEAD_DIM, HEAD_DIM), lambda i: (0, 0, 0)),
                     pl.BlockSpec((CONV_W - 1, d), lambda i: (0, 0)),
                     *[slabs(w, tile_steps, tile_idx) for w in weights], *ff_slabs)
        conv_rows = rows + HIST_ROWS
    else:
        ret_spec = pl.BlockSpec((n_seqs, HEADS, HEAD_DIM, HEAD_DIM), lambda i: (i, 0, 0, 0))
        conv_spec = pl.BlockSpec((n_seqs, CONV_W - 1, d), lambda i: (i, 0, 0))
        inputs = [x, state_ret, state_conv] + small + list(weights)
        in_specs = [tok, ret_spec, conv_spec] + small_specs + [_resident(w.shape) for w in weights]
        out_shape = (jax.ShapeDtypeStruct((n, d), F32),
                     jax.ShapeDtypeStruct(state_ret.shape, F32),
                     jax.ShapeDtypeStruct(state_conv.shape, F32))
        out_specs = (tok, ret_spec, conv_spec)
        conv_rows = n_seqs * (CHUNK + HIST_ROWS)
    scratch = [
        pltpu.VMEM((HEADS, tile, tile), F32),
        pltpu.VMEM((HEADS, tile, HEAD_DIM), F32),
        pltpu.VMEM((HEADS, tile, HEAD_DIM), F32),
        pltpu.VMEM((subtiles, tile, HEAD_DIM), F32),
        pltpu.VMEM((subtiles, tile, HEAD_DIM), F32),
        pltpu.VMEM((conv_rows, d), F32),
    ]
    if carry:
        scratch += [pltpu.VMEM((tile, HEAD_DIM), F32)] * 2
        scratch += [pltpu.VMEM(w.shape, BF16) for w in weights]
    return pl.pallas_call(
        functools.partial(_mixer_kernel, carry, tile, subtiles),
        out_shape=out_shape,
        grid=grid,
        in_specs=in_specs,
        out_specs=out_specs,
        scratch_shapes=scratch,
        compiler_params=pltpu.CompilerParams(
            dimension_semantics=("arbitrary",), vmem_limit_bytes=VMEM_LIMIT_BYTES),
        name="mixer_prompt" if carry else "mixer_sample",
    )(*inputs)


def _mlp_kernel(tile, subtiles, h_ref, n2_ref, w1_ref, w2_ref, nf_ref, y_ref):
    for sub in range(subtiles):
        rows = slice(sub * tile, (sub + 1) * tile)
        h = h_ref[rows, :]
        hn = _rms(h, n2_ref[...]).astype(BF16)
        a = jnp.maximum(_dot(hn, w1_ref[...]), 0.0)
        a2 = (a * a).astype(BF16)
        half = tile // 2
        for r in range(2):
            rs = slice(r * half, (r + 1) * half)
            o = h[rs] + _dot(a2[rs], w2_ref[...])
            y_ref[sub * tile + r * half:sub * tile + (r + 1) * half, :] = _rms(o, nf_ref[...])


def _mlp(h, n2, w1, w2, nf, name):
    n, d = h.shape
    tile, subtiles = MLP_TILE, MLP_SUBTILES
    tok = pl.BlockSpec((tile * subtiles, d), lambda i: (i, 0))
    weights = [n2, w1, w2, nf]
    return pl.pallas_call(
        functools.partial(_mlp_kernel, tile, subtiles),
        out_shape=jax.ShapeDtypeStruct((n, d), F32),
        grid=(n // (tile * subtiles),),
        in_specs=[tok] + [_resident(w.shape) for w in weights],
        out_specs=tok,
        compiler_params=pltpu.CompilerParams(
            dimension_semantics=("arbitrary",), vmem_limit_bytes=VMEM_LIMIT_BYTES),
        name=name,
    )(h, *weights)


def kernel(x_prompt, x_sample, state_ret, state_conv, norm1, w_in, ret_gn_w, conv_w, w_ret_out, w_conv_out, w_o, norm2, w_ff1, w_ff2, norm_f):
    depth = w_in.shape[0]
    assert depth == 1, "single-layer kernel"
    bp, lp, d = x_prompt.shape
    bs, ls, _ = x_sample.shape
    assert bp == 1 and ls == CHUNK
    assert lp % (MIXER_TILE * PROMPT_SUBTILES) == 0 and (bs * ls) % (MIXER_TILE * SAMPLE_SUBTILES) == 0
    assert lp % (MLP_TILE * MLP_SUBTILES) == 0 and (bs * ls) % (MLP_TILE * MLP_SUBTILES) == 0

    row = lambda v: v.reshape(1, -1)
    small = (row(norm1[0]), row(ret_gn_w[0]), conv_w[0])

    hp, ret_p, conv_p, *bf = _mixer(
        x_prompt.reshape(lp, d), None, None, *small,
        (w_in[0], w_ret_out[0], w_conv_out[0], w_o[0]), ff=(w_ff1[0], w_ff2[0]), carry=True)
    mixer_bf, (ff1_bf, ff2_bf) = bf[:4], bf[4:]
    mlp_w = (row(norm2[0]), ff1_bf, ff2_bf, row(norm_f))
    hs, ret_s, conv_s = _mixer(x_sample.reshape(bs * ls, d), state_ret[0], state_conv[0], *small, mixer_bf, carry=False)

    yp = _mlp(hp, *mlp_w, name="mlp_prompt").reshape(bp, lp, d)
    ys = _mlp(hs, *mlp_w, name="mlp_sample").reshape(bs, ls, d)
    return (yp, ys, ret_p[None, None], conv_p[None, None], ret_s[None], conv_s[None])
```

```python
import functools
import math

import jax
import jax.numpy as jnp
from jax import lax
from jax.experimental import pallas as pl
from jax.experimental.pallas import tpu as pltpu

CHUNK = 64
CHUNK_SHIFT = CHUNK.bit_length() - 1
assert 1 << CHUNK_SHIFT == CHUNK
HEADS = 8
HEAD_DIM = 128
CONV_W = 3
ROPE_BASE = 10000.0
EPS = 1e-6
PAST_LEN = 4096

F32_SUBLANES = 8
BF16_SUBLANES = 16
MXU_WIDTH = 256
VMEM_BYTES = 64 * 1024 * 1024

HIST_ROWS = F32_SUBLANES

MIXER_TILE = 256
PROMPT_SUBTILES = 2
SAMPLE_SUBTILES = 1
MLP_TILE = 512
MLP_SUBTILES = 2
CONVERT_STEPS = 16
VMEM_LIMIT_BYTES = VMEM_BYTES - 4 * 1024 * 1024

F32 = jnp.float32
BF16 = jnp.bfloat16


def _log_gamma(h):
    return math.log(1.0 - 2.0 ** (-5.0 - h))


def _rms(x, w):
    ms = jnp.mean(x * x, axis=-1, keepdims=True)
    return (x * lax.rsqrt(ms + EPS)) * w


def _dot(a, b):
    return jnp.dot(a, b, preferred_element_type=F32)


def _dot_nt(a, b):
    return lax.dot_general(a, b, (((1,), (1,)), ((), ())), preferred_element_type=F32)


def _dot_tn(a, b):
    return lax.dot_general(a, b, (((0,), (0,)), ((), ())), preferred_element_type=F32)


def _rope_cos_sin(pos):
    lane = lax.broadcasted_iota(jnp.int32, pos.shape, 1)
    half = HEAD_DIM // 2
    inv = jnp.exp((-math.log(ROPE_BASE)) * (lane & (half - 1)).astype(F32) / half)
    ang = pos * inv
    return jnp.cos(ang), jnp.where(lane < half, -1.0, 1.0) * jnp.sin(ang)


def _mixer_kernel(carry, tile, subtiles, *refs):
    n_chunks = tile // CHUNK
    if carry:
        (x_ref, n1_ref, gnw_ref, cw_ref, winf_ref, wrof_ref, wcof_ref, wof_ref, ff1_ref, ff2_ref,
         h_ref, sret_ref, sconv_ref, winb_ref, wrob_ref, wcob_ref, wob_ref, ff1_bf_ref, ff2_bf_ref,
         dmat, rowdec, coldec, cos_s, sin_s, ubuf,
         cosb_s, sinb_s, win_ref, wro_ref, wco_ref, wo_ref) = refs
        sin_ret_ref = sin_conv_ref = None
    else:
        (x_ref, sin_ret_ref, sin_conv_ref, n1_ref, gnw_ref, cw_ref, win_ref, wro_ref, wco_ref, wo_ref,
         h_ref, sret_ref, sconv_ref,
         dmat, rowdec, coldec, cos_s, sin_s, ubuf) = refs
    d_model = x_ref.shape[-1]
    step = pl.program_id(0)
    state_len = tile if carry else CHUNK

    @pl.when(step == 0)
    def _init():
        ii = lax.broadcasted_iota(jnp.int32, (tile, tile), 0)
        jj = lax.broadcasted_iota(jnp.int32, (tile, tile), 1)
        dist = jnp.abs(ii - jj).astype(F32)
        ci, cj = ii >> CHUNK_SHIFT, jj >> CHUNK_SHIFT
        keep = (cj <= ci) if carry else (cj == ci)
        row = lax.broadcasted_iota(jnp.int32, (tile, HEAD_DIM), 0)
        loc = (row if carry else row & (CHUNK - 1)).astype(F32)
        for h in range(HEADS):
            lg = _log_gamma(h)
            dmat[h] = jnp.where(keep, jnp.exp(lg * dist), 0.0)
            rowdec[h] = jnp.exp(lg * (loc + 1.0))
            coldec[h] = jnp.exp(lg * (state_len - 1.0 - loc))
        if carry:
            cosb_s[...], sinb_s[...] = _rope_cos_sin(row.astype(F32))
            sret_ref[...] = jnp.zeros_like(sret_ref)
            ubuf[0:HIST_ROWS, :] = jnp.zeros((HIST_ROWS, d_model), F32)
        else:
            cos, sin = _rope_cos_sin((PAST_LEN + (row & (CHUNK - 1))).astype(F32))
            for sub in range(subtiles):
                cos_s[sub], sin_s[sub] = cos, sin

    def tiles(t):
        w0, w1, w2 = cw_ref[0:1, :], cw_ref[1:2, :], cw_ref[2:3, :]

        for sub in range(subtiles):
            rows = slice(sub * tile, (sub + 1) * tile)
            x = x_ref[rows, :]
            xn = _rms(x, n1_ref[...]).astype(BF16)

            def proj(g, cs=None):
                lo, hi = (0, d_model) if cs is None else (cs.start, cs.stop)
                return _dot(xn, win_ref[:, g * d_model + lo:g * d_model + hi])

            slabs = [slice(c, c + MXU_WIDTH) for c in range(0, d_model, MXU_WIDTH)]

            if carry:
                first = jnp.zeros((F32_SUBLANES, HEAD_DIM), jnp.int32) + (t * subtiles + sub) * tile
                ct, st = _rope_cos_sin(first.astype(F32))
                ct, st = ct[0:1, :], st[0:1, :]
                cb, sb = cosb_s[...], sinb_s[...]
                cos_s[sub] = cb * ct - sb * st
                sin_s[sub] = sb * ct + cb * st

            def rope(z):
                return z * cos_s[sub] + pltpu.roll(z, HEAD_DIM // 2, 1) * sin_s[sub]

            def head_cols(g):
                for cs in slabs:
                    z = proj(g, cs)
                    for c in range(0, MXU_WIDTH, HEAD_DIM):
                        yield z[:, c:c + HEAD_DIM]

            qb = [rope(z).astype(BF16) for z in head_cols(0)]
            kb, kdb = [], []
            for h, z in enumerate(head_cols(1)):
                kr = rope(z) * (HEAD_DIM ** -0.5)
                kb.append(kr.astype(BF16))
                kdb.append((kr * coldec[h]).astype(BF16))
            vb = jnp.concatenate([proj(2, cs).astype(BF16) for cs in slabs], axis=1)

            def scores(h):
                return (_dot_nt(qb[h], kb[h]) * dmat[h]).astype(BF16)

            p_next = scores(0)
            gated = []
            for h in range(HEADS):
                hs = slice(h * HEAD_DIM, (h + 1) * HEAD_DIM)
                decay = math.exp(_log_gamma(h) * state_len)
                p = p_next
                if h + 1 < HEADS:
                    p_next = scores(h + 1)
                if h % 2 == 0:
                    sg = jax.nn.silu(proj(3, slabs[h // 2]))
                lanes = slice((h % 2) * HEAD_DIM, (h % 2 + 1) * HEAD_DIM)
                qh, kdh, vh = qb[h], kdb[h], vb[:, hs]
                o = _dot(p, vh)
                if carry:
                    s0 = sret_ref[h]
                    o = o + _dot(qh, s0.astype(BF16)) * rowdec[h]
                    sret_ref[h] = s0 * decay + _dot_tn(kdh, vh)
                else:
                    inter = []
                    for c in range(n_chunks):
                        rs = slice(c * CHUNK, (c + 1) * CHUNK)
                        seq = sub * n_chunks + c
                        s0 = sin_ret_ref[seq, h]
                        inter.append(_dot(qh[rs], s0.astype(BF16)))
                        sret_ref[seq, h] = s0 * decay + _dot_tn(kdh[rs], vh[rs])
                    o = o + jnp.concatenate(inter, axis=0) * rowdec[h]
                mu = jnp.mean(o, axis=-1, keepdims=True)
                d = o - mu
                var = jnp.mean(d * d, axis=-1, keepdims=True)
                on = (d * lax.rsqrt(var + EPS)) * gnw_ref[:, hs]
                gated.append((on * sg[:, lanes]).astype(BF16))
            gated = jnp.concatenate(gated, axis=1)

            u = jnp.concatenate([proj(5, cs) * proj(6, cs) for cs in slabs], axis=1)

            def taps(base, n, cs):
                return ((w0[:, cs] * ubuf[base - 2:base - 2 + n, cs]
                         + w1[:, cs] * ubuf[base - 1:base - 1 + n, cs]) + w2[:, cs] * ubuf[base:base + n, cs])

            if carry:
                base = HIST_ROWS + sub * tile
                ubuf[base:base + tile, :] = u
                conv = lambda cs: taps(base, tile, cs)
            else:
                bases = []
                for c in range(n_chunks):
                    seq = sub * n_chunks + c
                    cbase = seq * (CHUNK + HIST_ROWS) + HIST_ROWS
                    ubuf[cbase - (CONV_W - 1):cbase, :] = sin_conv_ref[seq]
                    ubuf[cbase:cbase + CHUNK, :] = u[c * CHUNK:(c + 1) * CHUNK]
                    sconv_ref[seq] = ubuf[cbase + CHUNK - (CONV_W - 1):cbase + CHUNK, :]
                    bases.append(cbase)
                conv = lambda cs: jnp.concatenate([taps(b, CHUNK, cs) for b in bases], axis=0)
            conv_in = jnp.concatenate([(proj(4, cs) * conv(cs)).astype(BF16) for cs in slabs], axis=1)

            mix = jnp.concatenate(
                [(jax.nn.sigmoid(proj(7, cs)) * _dot(gated, wro_ref[:, cs])
                  + jax.nn.sigmoid(proj(8, cs)) * _dot(conv_in, wco_ref[:, cs])).astype(BF16)
                 for cs in slabs], axis=1)
            h_ref[rows, :] = x_ref[rows, :] + _dot(mix, wo_ref[...])

        if carry:
            end = subtiles * tile + HIST_ROWS
            sconv_ref[...] = ubuf[end - (CONV_W - 1):end, :]
            ubuf[0:HIST_ROWS, :] = ubuf[end - HIST_ROWS:end, :]

    if carry:
        n_conv = win_ref.shape[0] // winf_ref.shape[0]

        @pl.when(step < n_conv)
        def _convert():
            for f_ref, w_ref in ((winf_ref, win_ref), (wrof_ref, wro_ref),
                                 (wcof_ref, wco_ref), (wof_ref, wo_ref)):
                slab = f_ref.shape[0]
                w_ref[pl.ds(pl.multiple_of(step * slab, slab), slab), :] = f_ref[...].astype(BF16)

        @pl.when(step >= n_conv)
        def _tiles():
            t = step - n_conv
            ff1_bf_ref[...] = ff1_ref[...].astype(BF16)
            ff2_bf_ref[...] = ff2_ref[...].astype(BF16)
            for b_ref, w_ref in ((winb_ref, win_ref), (wrob_ref, wro_ref),
                                 (wcob_ref, wco_ref), (wob_ref, wo_ref)):
                slab = b_ref.shape[0]
                b_ref[...] = w_ref[pl.ds(pl.multiple_of(t * slab, slab), slab), :]
            tiles(t)
    else:
        tiles(step)


def _resident(shape):
    nd = len(shape)
    return pl.BlockSpec(shape, lambda i: (0,) * nd, pipeline_mode=pl.Buffered(1))


def _mixer(x, state_ret, state_conv, n1, gnw, cw, weights, ff=None, *, carry):
    n, d = x.shape
    tile = MIXER_TILE
    subtiles = PROMPT_SUBTILES if carry else SAMPLE_SUBTILES
    rows = tile * subtiles
    n_seqs = rows // CHUNK
    tile_steps = n // rows
    n_conv = CONVERT_STEPS if carry else 0
    grid = (n_conv + tile_steps,)
    tile_idx = lambda i: jnp.maximum(i - n_conv, 0)
    tok = pl.BlockSpec((rows, d), lambda i: (tile_idx(i), 0))
    small = [n1, gnw, cw]
    small_specs = [_resident(w.shape) for w in small]

    def slabs(w, steps, index):
        slab = w.shape[0] // steps
        assert slab * steps == w.shape[0] and slab % BF16_SUBLANES == 0
        return pl.BlockSpec((slab, w.shape[1]), lambda i: (index(i), 0))

    if carry:
        conv_idx = lambda i: jnp.minimum(i, n_conv - 1)
        w_slabs = [slabs(w, n_conv, conv_idx) for w in weights]
        ff_slabs = [slabs(w, tile_steps, tile_idx) for w in ff]
        inputs = [x] + small + list(weights) + list(ff)
        in_specs = [tok] + small_specs + w_slabs + ff_slabs
        out_shape = (jax.ShapeDtypeStruct((n, d), F32),
                     jax.ShapeDtypeStruct((HEADS, HEAD_DIM, HEAD_DIM), F32),
                     jax.ShapeDtypeStruct((CONV_W - 1, d), F32),
                     *[jax.ShapeDtypeStruct(w.shape, BF16) for w in (*weights, *ff)])
        out_specs = (tok,
                     pl.BlockSpec((HEADS, HEAD_DIM, HEAD_DIM), lambda i: (0, 0, 0)),
                     pl.BlockSpec((CONV_W - 1, d), lambda i: (0, 0)),
                     *[slabs(w, tile_steps, tile_idx) for w in weights], *ff_slabs)
        conv_rows = rows + HIST_ROWS
    else:
        ret_spec = pl.BlockSpec((n_seqs, HEADS, HEAD_DIM, HEAD_DIM), lambda i: (i, 0, 0, 0))
        conv_spec = pl.BlockSpec((n_seqs, CONV_W - 1, d), lambda i: (i, 0, 0))
        inputs = [x, state_ret, state_conv] + small + list(weights)
        in_specs = [tok, ret_spec, conv_spec] + small_specs + [_resident(w.shape) for w in weights]
        out_shape = (jax.ShapeDtypeStruct((n, d), F32),
                     jax.ShapeDtypeStruct(state_ret.shape, F32),
                     jax.ShapeDtypeStruct(state_conv.shape, F32))
        out_specs = (tok, ret_spec, conv_spec)
        conv_rows = n_seqs * (CHUNK + HIST_ROWS)
    scratch = [
        pltpu.VMEM((HEADS, tile, tile), F32),
        pltpu.VMEM((HEADS, tile, HEAD_DIM), F32),
        pltpu.VMEM((HEADS, tile, HEAD_DIM), F32),
        pltpu.VMEM((subtiles, tile, HEAD_DIM), F32),
        pltpu.VMEM((subtiles, tile, HEAD_DIM), F32),
        pltpu.VMEM((conv_rows, d), F32),
    ]
    if carry:
        scratch += [pltpu.VMEM((tile, HEAD_DIM), F32)] * 2
        scratch += [pltpu.VMEM(w.shape, BF16) for w in weights]
    return pl.pallas_call(
        functools.partial(_mixer_kernel, carry, tile, subtiles),
        out_shape=out_shape,
        grid=grid,
        in_specs=in_specs,
        out_specs=out_specs,
        scratch_shapes=scratch,
        compiler_params=pltpu.CompilerParams(
            dimension_semantics=("arbitrary",), vmem_limit_bytes=VMEM_LIMIT_BYTES),
        name="mixer_prompt" if carry else "mixer_sample",
    )(*inputs)


def _mlp_kernel(tile, subtiles, h_ref, n2_ref, w1_ref, w2_ref, nf_ref, y_ref):
    for sub in range(subtiles):
        rows = slice(sub * tile, (sub + 1) * tile)
        h = h_ref[rows, :]
        hn = _rms(h, n2_ref[...]).astype(BF16)
        a = jnp.maximum(_dot(hn, w1_ref[...]), 0.0)
        a2 = (a * a).astype(BF16)
        half = tile // 2
        for r in range(2):
            rs = slice(r * half, (r + 1) * half)
            o = h[rs] + _dot(a2[rs], w2_ref[...])
            y_ref[sub * tile + r * half:sub * tile + (r + 1) * half, :] = _rms(o, nf_ref[...])


def _mlp(h, n2, w1, w2, nf, name):
    n, d = h.shape
    tile, subtiles = MLP_TILE, MLP_SUBTILES
    tok = pl.BlockSpec((tile * subtiles, d), lambda i: (i, 0))
    weights = [n2, w1, w2, nf]
    return pl.pallas_call(
        functools.partial(_mlp_kernel, tile, subtiles),
        out_shape=jax.ShapeDtypeStruct((n, d), F32),
        grid=(n // (tile * subtiles),),
        in_specs=[tok] + [_resident(w.shape) for w in weights],
        out_specs=tok,
        compiler_params=pltpu.CompilerParams(
            dimension_semantics=("arbitrary",), vmem_limit_bytes=VMEM_LIMIT_BYTES),
        name=name,
    )(h, *weights)


def kernel(x_prompt, x_sample, state_ret, state_conv, norm1, w_in, ret_gn_w, conv_w, w_ret_out, w_conv_out, w_o, norm2, w_ff1, w_ff2, norm_f):
    depth = w_in.shape[0]
    assert depth == 1, "single-layer kernel"
    bp, lp, d = x_prompt.shape
    bs, ls, _ = x_sample.shape
    assert bp == 1 and ls == CHUNK
    assert lp % (MIXER_TILE * PROMPT_SUBTILES) == 0 and (bs * ls) % (MIXER_TILE * SAMPLE_SUBTILES) == 0
    assert lp % (MLP_TILE * MLP_SUBTILES) == 0 and (bs * ls) % (MLP_TILE * MLP_SUBTILES) == 0

    row = lambda v: v.reshape(1, -1)
    small = (row(norm1[0]), row(ret_gn_w[0]), conv_w[0])

    hp, ret_p, conv_p, *bf = _mixer(
        x_prompt.reshape(lp, d), None, None, *small,
        (w_in[0], w_ret_out[0], w_conv_out[0], w_o[0]), ff=(w_ff1[0], w_ff2[0]), carry=True)
    mixer_bf, (ff1_bf, ff2_bf) = bf[:4], bf[4:]
    mlp_w = (row(norm2[0]), ff1_bf, ff2_bf, row(norm_f))
    hs, ret_s, conv_s = _mixer(x_sample.reshape(bs * ls, d), state_ret[0], state_conv[0], *small, mixer_bf, carry=False)

    yp = _mlp(hp, *mlp_w, name="mlp_prompt").reshape(bp, lp, d)
    ys = _mlp(hs, *mlp_w, name="mlp_sample").reshape(bs, ls, d)
    return (yp, ys, ret_p[None, None], conv_p[None, None], ret_s[None], conv_s[None])
```

```python
import functools
import math

import jax
import jax.numpy as jnp
from jax import lax
from jax.experimental import pallas as pl
from jax.experimental.pallas import tpu as pltpu

CHUNK = 64
CHUNK_SHIFT = CHUNK.bit_length() - 1
assert 1 << CHUNK_SHIFT == CHUNK
HEADS = 8
HEAD_DIM = 128
CONV_W = 3
ROPE_BASE = 10000.0
EPS = 1e-6
PAST_LEN = 4096

F32_SUBLANES = 8
BF16_SUBLANES = 16
MXU_WIDTH = 256
VMEM_BYTES = 64 * 1024 * 1024

HIST_ROWS = F32_SUBLANES

MIXER_TILE = 256
PROMPT_SUBTILES = 2
SAMPLE_SUBTILES = 1
MLP_TILE = 512
MLP_SUBTILES = 2
CONVERT_STEPS = 8
VMEM_LIMIT_BYTES = VMEM_BYTES - 1 * 1024 * 1024

F32 = jnp.float32
BF16 = jnp.bfloat16


def _log_gamma(h):
    return math.log(1.0 - 2.0 ** (-5.0 - h))


def _rms(x, w):
    ms = jnp.mean(x * x, axis=-1, keepdims=True)
    return (x * lax.rsqrt(ms + EPS)) * w


def _dot(a, b):
    return jnp.dot(a, b, preferred_element_type=F32)


def _dot_nt(a, b):
    return lax.dot_general(a, b, (((1,), (1,)), ((), ())), preferred_element_type=F32)


def _dot_tn(a, b):
    return lax.dot_general(a, b, (((0,), (0,)), ((), ())), preferred_element_type=F32)


def _rope_cos_sin(pos):
    lane = lax.broadcasted_iota(jnp.int32, pos.shape, 1)
    half = HEAD_DIM // 2
    inv = jnp.exp((-math.log(ROPE_BASE)) * (lane & (half - 1)).astype(F32) / half)
    ang = pos * inv
    return jnp.cos(ang), jnp.where(lane < half, -1.0, 1.0) * jnp.sin(ang)


def _mixer_kernel(carry, tile, subtiles, *refs):
    n_chunks = tile // CHUNK
    if carry:
        (x_ref, n1_ref, gnw_ref, cw_ref, winf_ref, wrof_ref, wcof_ref, wof_ref, ff1_ref, ff2_ref,
         h_ref, sret_ref, sconv_ref, winb_ref, wrob_ref, wcob_ref, wob_ref, ff1_bf_ref, ff2_bf_ref,
         dmat, rowdec, coldec, cos_s, sin_s, ubuf,
         cosb_s, sinb_s, win_ref, wro_ref, wco_ref, wo_ref) = refs
        sin_ret_ref = sin_conv_ref = None
    else:
        (x_ref, sin_ret_ref, sin_conv_ref, n1_ref, gnw_ref, cw_ref, win_ref, wro_ref, wco_ref, wo_ref,
         h_ref, sret_ref, sconv_ref,
         dmat, rowdec, coldec, cos_s, sin_s, ubuf) = refs
    d_model = x_ref.shape[-1]
    step = pl.program_id(0)
    state_len = tile if carry else CHUNK

    @pl.when(step == 0)
    def _init():
        ii = lax.broadcasted_iota(jnp.int32, (tile, tile), 0)
        jj = lax.broadcasted_iota(jnp.int32, (tile, tile), 1)
        dist = jnp.abs(ii - jj).astype(F32)
        ci, cj = ii >> CHUNK_SHIFT, jj >> CHUNK_SHIFT
        keep = (cj <= ci) if carry else (cj == ci)
        row = lax.broadcasted_iota(jnp.int32, (tile, HEAD_DIM), 0)
        loc = (row if carry else row & (CHUNK - 1)).astype(F32)
        for h in range(HEADS):
            lg = _log_gamma(h)
            dmat[h] = jnp.where(keep, jnp.exp(lg * dist), 0.0)
            rowdec[h] = jnp.exp(lg * (loc + 1.0))
            coldec[h] = jnp.exp(lg * (state_len - 1.0 - loc))
        if carry:
            cosb_s[...], sinb_s[...] = _rope_cos_sin(row.astype(F32))
            sret_ref[...] = jnp.zeros_like(sret_ref)
            ubuf[0:HIST_ROWS, :] = jnp.zeros((HIST_ROWS, d_model), F32)
        else:
            cos, sin = _rope_cos_sin((PAST_LEN + (row & (CHUNK - 1))).astype(F32))
            for sub in range(subtiles):
                cos_s[sub], sin_s[sub] = cos, sin

    def tiles(t):
        w0, w1, w2 = cw_ref[0:1, :], cw_ref[1:2, :], cw_ref[2:3, :]

        for sub in range(subtiles):
            rows = slice(sub * tile, (sub + 1) * tile)
            x = x_ref[rows, :]
            xn = _rms(x, n1_ref[...]).astype(BF16)

            def proj(g, cs=None):
                lo, hi = (0, d_model) if cs is None else (cs.start, cs.stop)
                return _dot(xn, win_ref[:, g * d_model + lo:g * d_model + hi])

            slabs = [slice(c, c + MXU_WIDTH) for c in range(0, d_model, MXU_WIDTH)]

            if carry:
                first = jnp.zeros((F32_SUBLANES, HEAD_DIM), jnp.int32) + (t * subtiles + sub) * tile
                ct, st = _rope_cos_sin(first.astype(F32))
                ct, st = ct[0:1, :], st[0:1, :]
                cb, sb = cosb_s[...], sinb_s[...]
                cos_s[sub] = cb * ct - sb * st
                sin_s[sub] = sb * ct + cb * st

            def rope(z):
                return z * cos_s[sub] + pltpu.roll(z, HEAD_DIM // 2, 1) * sin_s[sub]

            def head_cols(g):
                for cs in slabs:
                    z = proj(g, cs)
                    for c in range(0, MXU_WIDTH, HEAD_DIM):
                        yield z[:, c:c + HEAD_DIM]

            qb = [rope(z).astype(BF16) for z in head_cols(0)]
            kb, kdb = [], []
            for h, z in enumerate(head_cols(1)):
                kr = rope(z) * (HEAD_DIM ** -0.5)
                kb.append(kr.astype(BF16))
                kdb.append((kr * coldec[h]).astype(BF16))
            vb = jnp.concatenate([proj(2, cs).astype(BF16) for cs in slabs], axis=1)

            def scores(h):
                return (_dot_nt(qb[h], kb[h]) * dmat[h]).astype(BF16)

            p_next = scores(0)
            gated = []
            for h in range(HEADS):
                hs = slice(h * HEAD_DIM, (h + 1) * HEAD_DIM)
                decay = math.exp(_log_gamma(h) * state_len)
                p = p_next
                if h + 1 < HEADS:
                    p_next = scores(h + 1)
                if h % 2 == 0:
                    sg = jax.nn.silu(proj(3, slabs[h // 2]))
                lanes = slice((h % 2) * HEAD_DIM, (h % 2 + 1) * HEAD_DIM)
                qh, kdh, vh = qb[h], kdb[h], vb[:, hs]
                o = _dot(p, vh)
                if carry:
                    s0 = sret_ref[h]
                    o = o + _dot(qh, s0.astype(BF16)) * rowdec[h]
                    sret_ref[h] = s0 * decay + _dot_tn(kdh, vh)
                else:
                    inter = []
                    for c in range(n_chunks):
                        rs = slice(c * CHUNK, (c + 1) * CHUNK)
                        seq = sub * n_chunks + c
                        s0 = sin_ret_ref[seq, h]
                        inter.append(_dot(qh[rs], s0.astype(BF16)))
                        sret_ref[seq, h] = s0 * decay + _dot_tn(kdh[rs], vh[rs])
                    o = o + jnp.concatenate(inter, axis=0) * rowdec[h]
                mu = jnp.mean(o, axis=-1, keepdims=True)
                d = o - mu
                var = jnp.mean(d * d, axis=-1, keepdims=True)
                on = (d * lax.rsqrt(var + EPS)) * gnw_ref[:, hs]
                gated.append((on * sg[:, lanes]).astype(BF16))
            gated = jnp.concatenate(gated, axis=1)

            u = jnp.concatenate([proj(5, cs) * proj(6, cs) for cs in slabs], axis=1)

            def taps(base, n, cs):
                return ((w0[:, cs] * ubuf[base - 2:base - 2 + n, cs]
                         + w1[:, cs] * ubuf[base - 1:base - 1 + n, cs]) + w2[:, cs] * ubuf[base:base + n, cs])

            if carry:
                base = HIST_ROWS + sub * tile
                ubuf[base:base + tile, :] = u
                conv = lambda cs: taps(base, tile, cs)
            else:
                bases = []
                for c in range(n_chunks):
                    seq = sub * n_chunks + c
                    cbase = seq * (CHUNK + HIST_ROWS) + HIST_ROWS
                    ubuf[cbase - (CONV_W - 1):cbase, :] = sin_conv_ref[seq]
                    ubuf[cbase:cbase + CHUNK, :] = u[c * CHUNK:(c + 1) * CHUNK]
                    sconv_ref[seq] = ubuf[cbase + CHUNK - (CONV_W - 1):cbase + CHUNK, :]
                    bases.append(cbase)
                conv = lambda cs: jnp.concatenate([taps(b, CHUNK, cs) for b in bases], axis=0)
            conv_in = jnp.concatenate([(proj(4, cs) * conv(cs)).astype(BF16) for cs in slabs], axis=1)

            mix = jnp.concatenate(
                [(jax.nn.sigmoid(proj(7, cs)) * _dot(gated, wro_ref[:, cs])
                  + jax.nn.sigmoid(proj(8, cs)) * _dot(conv_in, wco_ref[:, cs])).astype(BF16)
                 for cs in slabs], axis=1)
            h_ref[rows, :] = x_ref[rows, :] + _dot(mix, wo_ref[...])

        if carry:
            end = subtiles * tile + HIST_ROWS
            sconv_ref[...] = ubuf[end - (CONV_W - 1):end, :]
            ubuf[0:HIST_ROWS, :] = ubuf[end - HIST_ROWS:end, :]

    if carry:
        n_conv = win_ref.shape[0] // winf_ref.shape[0]

        @pl.when(step < n_conv)
        def _convert():
            for f_ref, w_ref in ((winf_ref, win_ref), (wrof_ref, wro_ref),
                                 (wcof_ref, wco_ref), (wof_ref, wo_ref)):
                slab = f_ref.shape[0]
                w_ref[pl.ds(pl.multiple_of(step * slab, slab), slab), :] = f_ref[...].astype(BF16)

        @pl.when(step >= n_conv)
        def _tiles():
            t = step - n_conv
            ff1_bf_ref[...] = ff1_ref[...].astype(BF16)
            ff2_bf_ref[...] = ff2_ref[...].astype(BF16)
            for b_ref, w_ref in ((winb_ref, win_ref), (wrob_ref, wro_ref),
                                 (wcob_ref, wco_ref), (wob_ref, wo_ref)):
                slab = b_ref.shape[0]
                b_ref[...] = w_ref[pl.ds(pl.multiple_of(t * slab, slab), slab), :]
            tiles(t)
    else:
        tiles(step)


def _resident(shape):
    nd = len(shape)
    return pl.BlockSpec(shape, lambda i: (0,) * nd, pipeline_mode=pl.Buffered(1))


def _mixer(x, state_ret, state_conv, n1, gnw, cw, weights, ff=None, *, carry):
    n, d = x.shape
    tile = MIXER_TILE
    subtiles = PROMPT_SUBTILES if carry else SAMPLE_SUBTILES
    rows = tile * subtiles
    n_seqs = rows // CHUNK
    tile_steps = n // rows
    n_conv = CONVERT_STEPS if carry else 0
    grid = (n_conv + tile_steps,)
    tile_idx = lambda i: jnp.maximum(i - n_conv, 0)
    tok = pl.BlockSpec((rows, d), lambda i: (tile_idx(i), 0))
    small = [n1, gnw, cw]
    small_specs = [_resident(w.shape) for w in small]

    def slabs(w, steps, index):
        slab = w.shape[0] // steps
        assert slab * steps == w.shape[0] and slab % BF16_SUBLANES == 0
        return pl.BlockSpec((slab, w.shape[1]), lambda i: (index(i), 0))

    if carry:
        conv_idx = lambda i: jnp.minimum(i, n_conv - 1)
        w_slabs = [slabs(w, n_conv, conv_idx) for w in weights]
        ff_slabs = [slabs(w, tile_steps, tile_idx) for w in ff]
        inputs = [x] + small + list(weights) + list(ff)
        in_specs = [tok] + small_specs + w_slabs + ff_slabs
        out_shape = (jax.ShapeDtypeStruct((n, d), F32),
                     jax.ShapeDtypeStruct((HEADS, HEAD_DIM, HEAD_DIM), F32),
                     jax.ShapeDtypeStruct((CONV_W - 1, d), F32),
                     *[jax.ShapeDtypeStruct(w.shape, BF16) for w in (*weights, *ff)])
        out_specs = (tok,
                     pl.BlockSpec((HEADS, HEAD_DIM, HEAD_DIM), lambda i: (0, 0, 0)),
                     pl.BlockSpec((CONV_W - 1, d), lambda i: (0, 0)),
                     *[slabs(w, tile_steps, tile_idx) for w in weights], *ff_slabs)
        conv_rows = rows + HIST_ROWS
    else:
        ret_spec = pl.BlockSpec((n_seqs, HEADS, HEAD_DIM, HEAD_DIM), lambda i: (i, 0, 0, 0))
        conv_spec = pl.BlockSpec((n_seqs, CONV_W - 1, d), lambda i: (i, 0, 0))
        inputs = [x, state_ret, state_conv] + small + list(weights)
        in_specs = [tok, ret_spec, conv_spec] + small_specs + [_resident(w.shape) for w in weights]
        out_shape = (jax.ShapeDtypeStruct((n, d), F32),
                     jax.ShapeDtypeStruct(state_ret.shape, F32),
                     jax.ShapeDtypeStruct(state_conv.shape, F32))
        out_specs = (tok, ret_spec, conv_spec)
        conv_rows = n_seqs * (CHUNK + HIST_ROWS)
    scratch = [
        pltpu.VMEM((HEADS, tile, tile), F32),
        pltpu.VMEM((HEADS, tile, HEAD_DIM), F32),
        pltpu.VMEM((HEADS, tile, HEAD_DIM), F32),
        pltpu.VMEM((subtiles, tile, HEAD_DIM), F32),
        pltpu.VMEM((subtiles, tile, HEAD_DIM), F32),
        pltpu.VMEM((conv_rows, d), F32),
    ]
    if carry:
        scratch += [pltpu.VMEM((tile, HEAD_DIM), F32)] * 2
        scratch += [pltpu.VMEM(w.shape, BF16) for w in weights]
    return pl.pallas_call(
        functools.partial(_mixer_kernel, carry, tile, subtiles),
        out_shape=out_shape,
        grid=grid,
        in_specs=in_specs,
        out_specs=out_specs,
        scratch_shapes=scratch,
        compiler_params=pltpu.CompilerParams(
            dimension_semantics=("arbitrary",), vmem_limit_bytes=VMEM_LIMIT_BYTES),
        name="mixer_prompt" if carry else "mixer_sample",
    )(*inputs)


def _mlp_kernel(tile, subtiles, h_ref, n2_ref, w1_ref, w2_ref, nf_ref, y_ref):
    for sub in range(subtiles):
        rows = slice(sub * tile, (sub + 1) * tile)
        h = h_ref[rows, :]
        hn = _rms(h, n2_ref[...]).astype(BF16)
        a = jnp.maximum(_dot(hn, w1_ref[...]), 0.0)
        a2 = (a * a).astype(BF16)
        half = tile // 2
        for r in range(2):
            rs = slice(r * half, (r + 1) * half)
            o = h[rs] + _dot(a2[rs], w2_ref[...])
            y_ref[sub * tile + r * half:sub * tile + (r + 1) * half, :] = _rms(o, nf_ref[...])


def _mlp(h, n2, w1, w2, nf, name):
    n, d = h.shape
    tile, subtiles = MLP_TILE, MLP_SUBTILES
    tok = pl.BlockSpec((tile * subtiles, d), lambda i: (i, 0))
    weights = [n2, w1, w2, nf]
    return pl.pallas_call(
        functools.partial(_mlp_kernel, tile, subtiles),
        out_shape=jax.ShapeDtypeStruct((n, d), F32),
        grid=(n // (tile * subtiles),),
        in_specs=[tok] + [_resident(w.shape) for w in weights],
        out_specs=tok,
        compiler_params=pltpu.CompilerParams(
            dimension_semantics=("arbitrary",), vmem_limit_bytes=VMEM_LIMIT_BYTES),
        name=name,
    )(h, *weights)


def kernel(x_prompt, x_sample, state_ret, state_conv, norm1, w_in, ret_gn_w, conv_w, w_ret_out, w_conv_out, w_o, norm2, w_ff1, w_ff2, norm_f):
    depth = w_in.shape[0]
    assert depth == 1, "single-layer kernel"
    bp, lp, d = x_prompt.shape
    bs, ls, _ = x_sample.shape
    assert bp == 1 and ls == CHUNK
    assert lp % (MIXER_TILE * PROMPT_SUBTILES) == 0 and (bs * ls) % (MIXER_TILE * SAMPLE_SUBTILES) == 0
    assert lp % (MLP_TILE * MLP_SUBTILES) == 0 and (bs * ls) % (MLP_TILE * MLP_SUBTILES) == 0

    row = lambda v: v.reshape(1, -1)
    small = (row(norm1[0]), row(ret_gn_w[0]), conv_w[0])

    hp, ret_p, conv_p, *bf = _mixer(
        x_prompt.reshape(lp, d), None, None, *small,
        (w_in[0], w_ret_out[0], w_conv_out[0], w_o[0]), ff=(w_ff1[0], w_ff2[0]), carry=True)
    mixer_bf, (ff1_bf, ff2_bf) = bf[:4], bf[4:]
    mlp_w = (row(norm2[0]), ff1_bf, ff2_bf, row(norm_f))
    hs, ret_s, conv_s = _mixer(x_sample.reshape(bs * ls, d), state_ret[0], state_conv[0], *small, mixer_bf, carry=False)

    yp = _mlp(hp, *mlp_w, name="mlp_prompt").reshape(bp, lp, d)
    ys = _mlp(hs, *mlp_w, name="mlp_sample").reshape(bs, ls, d)
    return (yp, ys, ret_p[None, None], conv_p[None, None], ret_s[None], conv_s[None])
```

```python
import functools
import math

import jax
import jax.numpy as jnp
from jax import lax
from jax.experimental import pallas as pl
from jax.experimental.pallas import tpu as pltpu

CHUNK = 64
CHUNK_SHIFT = CHUNK.bit_length() - 1
assert 1 << CHUNK_SHIFT == CHUNK
HEADS = 8
HEAD_DIM = 128
CONV_W = 3
ROPE_BASE = 10000.0
EPS = 1e-6
PAST_LEN = 4096

F32_SUBLANES = 8
BF16_SUBLANES = 16
MXU_WIDTH = 256
VMEM_BYTES = 64 * 1024 * 1024

HIST_ROWS = F32_SUBLANES

MIXER_TILE = 256
PROMPT_SUBTILES = 2
SAMPLE_SUBTILES = 1
MLP_TILE = 512
MLP_SUBTILES = 2
CONVERT_STEPS = 8
VMEM_LIMIT_BYTES = VMEM_BYTES - 1024 * 1024

F32 = jnp.float32
BF16 = jnp.bfloat16


def _log_gamma(h):
    return math.log(1.0 - 2.0 ** (-5.0 - h))


def _rms(x, w):
    ms = jnp.mean(x * x, axis=-1, keepdims=True)
    return (x * lax.rsqrt(ms + EPS)) * w


def _dot(a, b):
    return jnp.dot(a, b, preferred_element_type=F32)


def _dot_nt(a, b):
    return lax.dot_general(a, b, (((1,), (1,)), ((), ())), preferred_element_type=F32)


def _dot_tn(a, b):
    return lax.dot_general(a, b, (((0,), (0,)), ((), ())), preferred_element_type=F32)


def _rope_cos_sin(pos):
    lane = lax.broadcasted_iota(jnp.int32, pos.shape, 1)
    half = HEAD_DIM // 2
    inv = jnp.exp((-math.log(ROPE_BASE)) * (lane & (half - 1)).astype(F32) / half)
    ang = pos * inv
    return jnp.cos(ang), jnp.where(lane < half, -1.0, 1.0) * jnp.sin(ang)


def _mixer_kernel(carry, tile, subtiles, *refs):
    n_chunks = tile // CHUNK
    if carry:
        (x_ref, n1_ref, gnw_ref, cw_ref, winf_ref, wrof_ref, wcof_ref, wof_ref, ff1_ref, ff2_ref,
         h_ref, sret_ref, sconv_ref, winb_ref, wrob_ref, wcob_ref, wob_ref, ff1_bf_ref, ff2_bf_ref,
         dmat, rowdec, coldec, cos_s, sin_s, ubuf,
         cosb_s, sinb_s, win_ref, wro_ref, wco_ref, wo_ref) = refs
        sin_ret_ref = sin_conv_ref = None
    else:
        (x_ref, sin_ret_ref, sin_conv_ref, n1_ref, gnw_ref, cw_ref, win_ref, wro_ref, wco_ref, wo_ref,
         h_ref, sret_ref, sconv_ref,
         dmat, rowdec, coldec, cos_s, sin_s, ubuf) = refs
    d_model = x_ref.shape[-1]
    step = pl.program_id(0)
    state_len = tile if carry else CHUNK

    @pl.when(step == 0)
    def _init():
        ii = lax.broadcasted_iota(jnp.int32, (tile, tile), 0)
        jj = lax.broadcasted_iota(jnp.int32, (tile, tile), 1)
        dist = jnp.abs(ii - jj).astype(F32)
        ci, cj = ii >> CHUNK_SHIFT, jj >> CHUNK_SHIFT
        keep = (cj <= ci) if carry else (cj == ci)
        row = lax.broadcasted_iota(jnp.int32, (tile, HEAD_DIM), 0)
        loc = (row if carry else row & (CHUNK - 1)).astype(F32)
        for h in range(HEADS):
            lg = _log_gamma(h)
            dmat[h] = jnp.where(keep, jnp.exp(lg * dist), 0.0)
            rowdec[h] = jnp.exp(lg * (loc + 1.0))
            coldec[h] = jnp.exp(lg * (state_len - 1.0 - loc))
        if carry:
            cosb_s[...], sinb_s[...] = _rope_cos_sin(row.astype(F32))
            sret_ref[...] = jnp.zeros_like(sret_ref)
            ubuf[0:HIST_ROWS, :] = jnp.zeros((HIST_ROWS, d_model), F32)
        else:
            cos, sin = _rope_cos_sin((PAST_LEN + (row & (CHUNK - 1))).astype(F32))
            for sub in range(subtiles):
                cos_s[sub], sin_s[sub] = cos, sin

    def tiles(t):
        w0, w1, w2 = cw_ref[0:1, :], cw_ref[1:2, :], cw_ref[2:3, :]

        for sub in range(subtiles):
            rows = slice(sub * tile, (sub + 1) * tile)
            x = x_ref[rows, :]
            xn = _rms(x, n1_ref[...]).astype(BF16)

            def proj(g, cs=None):
                lo, hi = (0, d_model) if cs is None else (cs.start, cs.stop)
                return _dot(xn, win_ref[:, g * d_model + lo:g * d_model + hi])

            slabs = [slice(c, c + MXU_WIDTH) for c in range(0, d_model, MXU_WIDTH)]

            if carry:
                first = jnp.zeros((F32_SUBLANES, HEAD_DIM), jnp.int32) + (t * subtiles + sub) * tile
                ct, st = _rope_cos_sin(first.astype(F32))
                ct, st = ct[0:1, :], st[0:1, :]
                cb, sb = cosb_s[...], sinb_s[...]
                cos_s[sub] = cb * ct - sb * st
                sin_s[sub] = sb * ct + cb * st

            def rope(z):
                return z * cos_s[sub] + pltpu.roll(z, HEAD_DIM // 2, 1) * sin_s[sub]

            def head_cols(g):
                for cs in slabs:
                    z = proj(g, cs)
                    for c in range(0, MXU_WIDTH, HEAD_DIM):
                        yield z[:, c:c + HEAD_DIM]

            qb = [rope(z).astype(BF16) for z in head_cols(0)]
            kb, kdb = [], []
            for h, z in enumerate(head_cols(1)):
                kr = rope(z) * (HEAD_DIM ** -0.5)
                kb.append(kr.astype(BF16))
                kdb.append((kr * coldec[h]).astype(BF16))
            vb = jnp.concatenate([proj(2, cs).astype(BF16) for cs in slabs], axis=1)

            def scores(h):
                return (_dot_nt(qb[h], kb[h]) * dmat[h]).astype(BF16)

            p_next = scores(0)
            gated = []
            for h in range(HEADS):
                hs = slice(h * HEAD_DIM, (h + 1) * HEAD_DIM)
                decay = math.exp(_log_gamma(h) * state_len)
                p = p_next
                if h + 1 < HEADS:
                    p_next = scores(h + 1)
                if h % 2 == 0:
                    sg = jax.nn.silu(proj(3, slabs[h // 2]))
                lanes = slice((h % 2) * HEAD_DIM, (h % 2 + 1) * HEAD_DIM)
                qh, kdh, vh = qb[h], kdb[h], vb[:, hs]
                o = _dot(p, vh)
                if carry:
                    s0 = sret_ref[h]
                    o = o + _dot(qh, s0.astype(BF16)) * rowdec[h]
                    sret_ref[h] = s0 * decay + _dot_tn(kdh, vh)
                else:
                    inter = []
                    for c in range(n_chunks):
                        rs = slice(c * CHUNK, (c + 1) * CHUNK)
                        seq = sub * n_chunks + c
                        s0 = sin_ret_ref[seq, h]
                        inter.append(_dot(qh[rs], s0.astype(BF16)))
                        sret_ref[seq, h] = s0 * decay + _dot_tn(kdh[rs], vh[rs])
                    o = o + jnp.concatenate(inter, axis=0) * rowdec[h]
                mu = jnp.mean(o, axis=-1, keepdims=True)
                d = o - mu
                var = jnp.mean(d * d, axis=-1, keepdims=True)
                on = (d * lax.rsqrt(var + EPS)) * gnw_ref[:, hs]
                gated.append((on * sg[:, lanes]).astype(BF16))
            gated = jnp.concatenate(gated, axis=1)

            u = jnp.concatenate([proj(5, cs) * proj(6, cs) for cs in slabs], axis=1)

            def taps(base, n, cs):
                return ((w0[:, cs] * ubuf[base - 2:base - 2 + n, cs]
                         + w1[:, cs] * ubuf[base - 1:base - 1 + n, cs]) + w2[:, cs] * ubuf[base:base + n, cs])

            if carry:
                base = HIST_ROWS + sub * tile
                ubuf[base:base + tile, :] = u
                conv = lambda cs: taps(base, tile, cs)
            else:
                bases = []
                for c in range(n_chunks):
                    seq = sub * n_chunks + c
                    cbase = seq * (CHUNK + HIST_ROWS) + HIST_ROWS
                    ubuf[cbase - (CONV_W - 1):cbase, :] = sin_conv_ref[seq]
                    ubuf[cbase:cbase + CHUNK, :] = u[c * CHUNK:(c + 1) * CHUNK]
                    sconv_ref[seq] = ubuf[cbase + CHUNK - (CONV_W - 1):cbase + CHUNK, :]
                    bases.append(cbase)
                conv = lambda cs: jnp.concatenate([taps(b, CHUNK, cs) for b in bases], axis=0)
            conv_in = jnp.concatenate([(proj(4, cs) * conv(cs)).astype(BF16) for cs in slabs], axis=1)

            mix = jnp.concatenate(
                [(jax.nn.sigmoid(proj(7, cs)) * _dot(gated, wro_ref[:, cs])
                  + jax.nn.sigmoid(proj(8, cs)) * _dot(conv_in, wco_ref[:, cs])).astype(BF16)
                 for cs in slabs], axis=1)
            h_ref[rows, :] = x_ref[rows, :] + _dot(mix, wo_ref[...])

        if carry:
            end = subtiles * tile + HIST_ROWS
            sconv_ref[...] = ubuf[end - (CONV_W - 1):end, :]
            ubuf[0:HIST_ROWS, :] = ubuf[end - HIST_ROWS:end, :]

    if carry:
        n_conv = win_ref.shape[0] // winf_ref.shape[0]

        @pl.when(step < n_conv)
        def _convert():
            for f_ref, w_ref in ((winf_ref, win_ref), (wrof_ref, wro_ref),
                                 (wcof_ref, wco_ref), (wof_ref, wo_ref)):
                slab = f_ref.shape[0]
                w_ref[pl.ds(pl.multiple_of(step * slab, slab), slab), :] = f_ref[...].astype(BF16)

        @pl.when(step >= n_conv)
        def _tiles():
            t = step - n_conv
            ff1_bf_ref[...] = ff1_ref[...].astype(BF16)
            ff2_bf_ref[...] = ff2_ref[...].astype(BF16)
            for b_ref, w_ref in ((winb_ref, win_ref), (wrob_ref, wro_ref),
                                 (wcob_ref, wco_ref), (wob_ref, wo_ref)):
                slab = b_ref.shape[0]
                b_ref[...] = w_ref[pl.ds(pl.multiple_of(t * slab, slab), slab), :]
            tiles(t)
    else:
        tiles(step)


def _resident(shape):
    nd = len(shape)
    return pl.BlockSpec(shape, lambda i: (0,) * nd, pipeline_mode=pl.Buffered(1))


def _mixer(x, state_ret, state_conv, n1, gnw, cw, weights, ff=None, *, carry):
    n, d = x.shape
    tile = MIXER_TILE
    subtiles = PROMPT_SUBTILES if carry else SAMPLE_SUBTILES
    rows = tile * subtiles
    n_seqs = rows // CHUNK
    tile_steps = n // rows
    n_conv = CONVERT_STEPS if carry else 0
    grid = (n_conv + tile_steps,)
    tile_idx = lambda i: jnp.maximum(i - n_conv, 0)
    tok = pl.BlockSpec((rows, d), lambda i: (tile_idx(i), 0))
    small = [n1, gnw, cw]
    small_specs = [_resident(w.shape) for w in small]

    def slabs(w, steps, index):
        slab = w.shape[0] // steps
        assert slab * steps == w.shape[0] and slab % BF16_SUBLANES == 0
        return pl.BlockSpec((slab, w.shape[1]), lambda i: (index(i), 0))

    if carry:
        conv_idx = lambda i: jnp.minimum(i, n_conv - 1)
        w_slabs = [slabs(w, n_conv, conv_idx) for w in weights]
        ff_slabs = [slabs(w, tile_steps, tile_idx) for w in ff]
        inputs = [x] + small + list(weights) + list(ff)
        in_specs = [tok] + small_specs + w_slabs + ff_slabs
        out_shape = (jax.ShapeDtypeStruct((n, d), F32),
                     jax.ShapeDtypeStruct((HEADS, HEAD_DIM, HEAD_DIM), F32),
                     jax.ShapeDtypeStruct((CONV_W - 1, d), F32),
                     *[jax.ShapeDtypeStruct(w.shape, BF16) for w in (*weights, *ff)])
        out_specs = (tok,
                     pl.BlockSpec((HEADS, HEAD_DIM, HEAD_DIM), lambda i: (0, 0, 0)),
                     pl.BlockSpec((CONV_W - 1, d), lambda i: (0, 0)),
                     *[slabs(w, tile_steps, tile_idx) for w in weights], *ff_slabs)
        conv_rows = rows + HIST_ROWS
    else:
        ret_spec = pl.BlockSpec((n_seqs, HEADS, HEAD_DIM, HEAD_DIM), lambda i: (i, 0, 0, 0))
        conv_spec = pl.BlockSpec((n_seqs, CONV_W - 1, d), lambda i: (i, 0, 0))
        inputs = [x, state_ret, state_conv] + small + list(weights)
        in_specs = [tok, ret_spec, conv_spec] + small_specs + [_resident(w.shape) for w in weights]
        out_shape = (jax.ShapeDtypeStruct((n, d), F32),
                     jax.ShapeDtypeStruct(state_ret.shape, F32),
                     jax.ShapeDtypeStruct(state_conv.shape, F32))
        out_specs = (tok, ret_spec, conv_spec)
        conv_rows = n_seqs * (CHUNK + HIST_ROWS)
    scratch = [
        pltpu.VMEM((HEADS, tile, tile), F32),
        pltpu.VMEM((HEADS, tile, HEAD_DIM), F32),
        pltpu.VMEM((HEADS, tile, HEAD_DIM), F32),
        pltpu.VMEM((subtiles, tile, HEAD_DIM), F32),
        pltpu.VMEM((subtiles, tile, HEAD_DIM), F32),
        pltpu.VMEM((conv_rows, d), F32),
    ]
    if carry:
        scratch += [pltpu.VMEM((tile, HEAD_DIM), F32)] * 2
        scratch += [pltpu.VMEM(w.shape, BF16) for w in weights]
    return pl.pallas_call(
        functools.partial(_mixer_kernel, carry, tile, subtiles),
        out_shape=out_shape,
        grid=grid,
        in_specs=in_specs,
        out_specs=out_specs,
        scratch_shapes=scratch,
        compiler_params=pltpu.CompilerParams(
            dimension_semantics=("arbitrary",), vmem_limit_bytes=VMEM_LIMIT_BYTES),
        name="mixer_prompt" if carry else "mixer_sample",
    )(*inputs)


def _mlp_kernel(tile, subtiles, h_ref, n2_ref, w1_ref, w2_ref, nf_ref, y_ref):
    for sub in range(subtiles):
        rows = slice(sub * tile, (sub + 1) * tile)
        h = h_ref[rows, :]
        hn = _rms(h, n2_ref[...]).astype(BF16)
        a = jnp.maximum(_dot(hn, w1_ref[...]), 0.0)
        a2 = (a * a).astype(BF16)
        half = tile // 2
        for r in range(2):
            rs = slice(r * half, (r + 1) * half)
            o = h[rs] + _dot(a2[rs], w2_ref[...])
            y_ref[sub * tile + r * half:sub * tile + (r + 1) * half, :] = _rms(o, nf_ref[...])


def _mlp(h, n2, w1, w2, nf, name):
    n, d = h.shape
    tile, subtiles = MLP_TILE, MLP_SUBTILES
    tok = pl.BlockSpec((tile * subtiles, d), lambda i: (i, 0))
    weights = [n2, w1, w2, nf]
    return pl.pallas_call(
        functools.partial(_mlp_kernel, tile, subtiles),
        out_shape=jax.ShapeDtypeStruct((n, d), F32),
        grid=(n // (tile * subtiles),),
        in_specs=[tok] + [_resident(w.shape) for w in weights],
        out_specs=tok,
        compiler_params=pltpu.CompilerParams(
            dimension_semantics=("arbitrary",), vmem_limit_bytes=VMEM_LIMIT_BYTES),
        name=name,
    )(h, *weights)


def kernel(x_prompt, x_sample, state_ret, state_conv, norm1, w_in, ret_gn_w, conv_w, w_ret_out, w_conv_out, w_o, norm2, w_ff1, w_ff2, norm_f):
    depth = w_in.shape[0]
    assert depth == 1, "single-layer kernel"
    bp, lp, d = x_prompt.shape
    bs, ls, _ = x_sample.shape
    assert bp == 1 and ls == CHUNK
    assert lp % (MIXER_TILE * PROMPT_SUBTILES) == 0 and (bs * ls) % (MIXER_TILE * SAMPLE_SUBTILES) == 0
    assert lp % (MLP_TILE * MLP_SUBTILES) == 0 and (bs * ls) % (MLP_TILE * MLP_SUBTILES) == 0

    row = lambda v: v.reshape(1, -1)
    small = (row(norm1[0]), row(ret_gn_w[0]), conv_w[0])

    hp, ret_p, conv_p, *bf = _mixer(
        x_prompt.reshape(lp, d), None, None, *small,
        (w_in[0], w_ret_out[0], w_conv_out[0], w_o[0]), ff=(w_ff1[0], w_ff2[0]), carry=True)
    mixer_bf, (ff1_bf, ff2_bf) = bf[:4], bf[4:]
    mlp_w = (row(norm2[0]), ff1_bf, ff2_bf, row(norm_f))
    hs, ret_s, conv_s = _mixer(x_sample.reshape(bs * ls, d), state_ret[0], state_conv[0], *small, mixer_bf, carry=False)

    yp = _mlp(hp, *mlp_w, name="mlp_prompt").reshape(bp, lp, d)
    ys = _mlp(hs, *mlp_w, name="mlp_sample").reshape(bs, ls, d)
    return (yp, ys, ret_p[None, None], conv_p[None, None], ret_s[None], conv_s[None])
```

```python
import functools
import math

import jax
import jax.numpy as jnp
from jax import lax
from jax.experimental import pallas as pl
from jax.experimental.pallas import tpu as pltpu

CHUNK = 64
CHUNK_SHIFT = CHUNK.bit_length() - 1
assert 1 << CHUNK_SHIFT == CHUNK
HEADS = 8
HEAD_DIM = 128
CONV_W = 3
ROPE_BASE = 10000.0
EPS = 1e-6
PAST_LEN = 4096

F32_SUBLANES = 8
BF16_SUBLANES = 16
MXU_WIDTH = 256
VMEM_BYTES = 64 * 1024 * 1024

HIST_ROWS = F32_SUBLANES

MIXER_TILE = 256
PROMPT_SUBTILES = 2
SAMPLE_SUBTILES = 2
MLP_TILE = 512
MLP_SUBTILES = 2
CONVERT_STEPS = 8
VMEM_LIMIT_BYTES = VMEM_BYTES - 1024 * 1024

F32 = jnp.float32
BF16 = jnp.bfloat16


def _log_gamma(h):
    return math.log(1.0 - 2.0 ** (-5.0 - h))


def _rms(x, w):
    ms = jnp.mean(x * x, axis=-1, keepdims=True)
    return (x * lax.rsqrt(ms + EPS)) * w


def _dot(a, b):
    return jnp.dot(a, b, preferred_element_type=F32)


def _dot_nt(a, b):
    return lax.dot_general(a, b, (((1,), (1,)), ((), ())), preferred_element_type=F32)


def _dot_tn(a, b):
    return lax.dot_general(a, b, (((0,), (0,)), ((), ())), preferred_element_type=F32)


def _rope_cos_sin(pos):
    lane = lax.broadcasted_iota(jnp.int32, pos.shape, 1)
    half = HEAD_DIM // 2
    inv = jnp.exp((-math.log(ROPE_BASE)) * (lane & (half - 1)).astype(F32) / half)
    ang = pos * inv
    return jnp.cos(ang), jnp.where(lane < half, -1.0, 1.0) * jnp.sin(ang)


def _mixer_kernel(carry, tile, subtiles, *refs):
    n_chunks = tile // CHUNK
    if carry:
        (x_ref, n1_ref, gnw_ref, cw_ref, winf_ref, wrof_ref, wcof_ref, wof_ref, ff1_ref, ff2_ref,
         h_ref, sret_ref, sconv_ref, winb_ref, wrob_ref, wcob_ref, wob_ref, ff1_bf_ref, ff2_bf_ref,
         dmat, rowdec, coldec, cos_s, sin_s, ubuf,
         cosb_s, sinb_s, win_ref, wro_ref, wco_ref, wo_ref) = refs
        sin_ret_ref = sin_conv_ref = None
    else:
        (x_ref, sin_ret_ref, sin_conv_ref, n1_ref, gnw_ref, cw_ref, win_ref, wro_ref, wco_ref, wo_ref,
         h_ref, sret_ref, sconv_ref,
         dmat, rowdec, coldec, cos_s, sin_s, ubuf) = refs
    d_model = x_ref.shape[-1]
    step = pl.program_id(0)
    state_len = tile if carry else CHUNK

    @pl.when(step == 0)
    def _init():
        ii = lax.broadcasted_iota(jnp.int32, (tile, tile), 0)
        jj = lax.broadcasted_iota(jnp.int32, (tile, tile), 1)
        dist = jnp.abs(ii - jj).astype(F32)
        ci, cj = ii >> CHUNK_SHIFT, jj >> CHUNK_SHIFT
        keep = (cj <= ci) if carry else (cj == ci)
        row = lax.broadcasted_iota(jnp.int32, (tile, HEAD_DIM), 0)
        loc = (row if carry else row & (CHUNK - 1)).astype(F32)
        for h in range(HEADS):
            lg = _log_gamma(h)
            dmat[h] = jnp.where(keep, jnp.exp(lg * dist), 0.0)
            rowdec[h] = jnp.exp(lg * (loc + 1.0))
            coldec[h] = jnp.exp(lg * (state_len - 1.0 - loc))
        if carry:
            cosb_s[...], sinb_s[...] = _rope_cos_sin(row.astype(F32))
            sret_ref[...] = jnp.zeros_like(sret_ref)
            ubuf[0:HIST_ROWS, :] = jnp.zeros((HIST_ROWS, d_model), F32)
        else:
            cos, sin = _rope_cos_sin((PAST_LEN + (row & (CHUNK - 1))).astype(F32))
            for sub in range(subtiles):
                cos_s[sub], sin_s[sub] = cos, sin

    def tiles(t):
        w0, w1, w2 = cw_ref[0:1, :], cw_ref[1:2, :], cw_ref[2:3, :]

        for sub in range(subtiles):
            rows = slice(sub * tile, (sub + 1) * tile)
            x = x_ref[rows, :]
            xn = _rms(x, n1_ref[...]).astype(BF16)

            def proj(g, cs=None):
                lo, hi = (0, d_model) if cs is None else (cs.start, cs.stop)
                return _dot(xn, win_ref[:, g * d_model + lo:g * d_model + hi])

            slabs = [slice(c, c + MXU_WIDTH) for c in range(0, d_model, MXU_WIDTH)]

            if carry:
                first = jnp.zeros((F32_SUBLANES, HEAD_DIM), jnp.int32) + (t * subtiles + sub) * tile
                ct, st = _rope_cos_sin(first.astype(F32))
                ct, st = ct[0:1, :], st[0:1, :]
                cb, sb = cosb_s[...], sinb_s[...]
                cos_s[sub] = cb * ct - sb * st
                sin_s[sub] = sb * ct + cb * st

            def rope(z):
                return z * cos_s[sub] + pltpu.roll(z, HEAD_DIM // 2, 1) * sin_s[sub]

            def head_cols(g):
                for cs in slabs:
                    z = proj(g, cs)
                    for c in range(0, MXU_WIDTH, HEAD_DIM):
                        yield z[:, c:c + HEAD_DIM]

            qb = [rope(z).astype(BF16) for z in head_cols(0)]
            kb, kdb = [], []
            for h, z in enumerate(head_cols(1)):
                kr = rope(z) * (HEAD_DIM ** -0.5)
                kb.append(kr.astype(BF16))
                kdb.append((kr * coldec[h]).astype(BF16))
            vb = jnp.concatenate([proj(2, cs).astype(BF16) for cs in slabs], axis=1)

            def scores(h):
                return (_dot_nt(qb[h], kb[h]) * dmat[h]).astype(BF16)

            p_next = scores(0)
            gated = []
            for h in range(HEADS):
                hs = slice(h * HEAD_DIM, (h + 1) * HEAD_DIM)
                decay = math.exp(_log_gamma(h) * state_len)
                p = p_next
                if h + 1 < HEADS:
                    p_next = scores(h + 1)
                if h % 2 == 0:
                    sg = jax.nn.silu(proj(3, slabs[h // 2]))
                lanes = slice((h % 2) * HEAD_DIM, (h % 2 + 1) * HEAD_DIM)
                qh, kdh, vh = qb[h], kdb[h], vb[:, hs]
                o = _dot(p, vh)
                if carry:
                    s0 = sret_ref[h]
                    o = o + _dot(qh, s0.astype(BF16)) * rowdec[h]
                    sret_ref[h] = s0 * decay + _dot_tn(kdh, vh)
                else:
                    inter = []
                    for c in range(n_chunks):
                        rs = slice(c * CHUNK, (c + 1) * CHUNK)
                        seq = sub * n_chunks + c
                        s0 = sin_ret_ref[seq, h]
                        inter.append(_dot(qh[rs], s0.astype(BF16)))
                        sret_ref[seq, h] = s0 * decay + _dot_tn(kdh[rs], vh[rs])
                    o = o + jnp.concatenate(inter, axis=0) * rowdec[h]
                mu = jnp.mean(o, axis=-1, keepdims=True)
                d = o - mu
                var = jnp.mean(d * d, axis=-1, keepdims=True)
                on = (d * lax.rsqrt(var + EPS)) * gnw_ref[:, hs]
                gated.append((on * sg[:, lanes]).astype(BF16))
            gated = jnp.concatenate(gated, axis=1)

            u = jnp.concatenate([proj(5, cs) * proj(6, cs) for cs in slabs], axis=1)

            def taps(base, n, cs):
                return ((w0[:, cs] * ubuf[base - 2:base - 2 + n, cs]
                         + w1[:, cs] * ubuf[base - 1:base - 1 + n, cs]) + w2[:, cs] * ubuf[base:base + n, cs])

            if carry:
                base = HIST_ROWS + sub * tile
                ubuf[base:base + tile, :] = u
                conv = lambda cs: taps(base, tile, cs)
            else:
                bases = []
                for c in range(n_chunks):
                    seq = sub * n_chunks + c
                    cbase = seq * (CHUNK + HIST_ROWS) + HIST_ROWS
                    ubuf[cbase - (CONV_W - 1):cbase, :] = sin_conv_ref[seq]
                    ubuf[cbase:cbase + CHUNK, :] = u[c * CHUNK:(c + 1) * CHUNK]
                    sconv_ref[seq] = ubuf[cbase + CHUNK - (CONV_W - 1):cbase + CHUNK, :]
                    bases.append(cbase)
                conv = lambda cs: jnp.concatenate([taps(b, CHUNK, cs) for b in bases], axis=0)
            conv_in = jnp.concatenate([(proj(4, cs) * conv(cs)).astype(BF16) for cs in slabs], axis=1)

            mix = jnp.concatenate(
                [(jax.nn.sigmoid(proj(7, cs)) * _dot(gated, wro_ref[:, cs])
                  + jax.nn.sigmoid(proj(8, cs)) * _dot(conv_in, wco_ref[:, cs])).astype(BF16)
                 for cs in slabs], axis=1)
            h_ref[rows, :] = x_ref[rows, :] + _dot(mix, wo_ref[...])

        if carry:
            end = subtiles * tile + HIST_ROWS
            sconv_ref[...] = ubuf[end - (CONV_W - 1):end, :]
            ubuf[0:HIST_ROWS, :] = ubuf[end - HIST_ROWS:end, :]

    if carry:
        n_conv = win_ref.shape[0] // winf_ref.shape[0]

        @pl.when(step < n_conv)
        def _convert():
            for f_ref, w_ref in ((winf_ref, win_ref), (wrof_ref, wro_ref),
                                 (wcof_ref, wco_ref), (wof_ref, wo_ref)):
                slab = f_ref.shape[0]
                w_ref[pl.ds(pl.multiple_of(step * slab, slab), slab), :] = f_ref[...].astype(BF16)

        @pl.when(step >= n_conv)
        def _tiles():
            t = step - n_conv
            ff1_bf_ref[...] = ff1_ref[...].astype(BF16)
            ff2_bf_ref[...] = ff2_ref[...].astype(BF16)
            for b_ref, w_ref in ((winb_ref, win_ref), (wrob_ref, wro_ref),
                                 (wcob_ref, wco_ref), (wob_ref, wo_ref)):
                slab = b_ref.shape[0]
                b_ref[...] = w_ref[pl.ds(pl.multiple_of(t * slab, slab), slab), :]
            tiles(t)
    else:
        tiles(step)


def _resident(shape):
    nd = len(shape)
    return pl.BlockSpec(shape, lambda i: (0,) * nd, pipeline_mode=pl.Buffered(1))


def _mixer(x, state_ret, state_conv, n1, gnw, cw, weights, ff=None, *, carry):
    n, d = x.shape
    tile = MIXER_TILE
    subtiles = PROMPT_SUBTILES if carry else SAMPLE_SUBTILES
    rows = tile * subtiles
    n_seqs = rows // CHUNK
    tile_steps = n // rows
    n_conv = CONVERT_STEPS if carry else 0
    grid = (n_conv + tile_steps,)
    tile_idx = lambda i: jnp.maximum(i - n_conv, 0)
    tok = pl.BlockSpec((rows, d), lambda i: (tile_idx(i), 0))
    small = [n1, gnw, cw]
    small_specs = [_resident(w.shape) for w in small]

    def slabs(w, steps, index):
        slab = w.shape[0] // steps
        assert slab * steps == w.shape[0] and slab % BF16_SUBLANES == 0
        return pl.BlockSpec((slab, w.shape[1]), lambda i: (index(i), 0))

    if carry:
        conv_idx = lambda i: jnp.minimum(i, n_conv - 1)
        w_slabs = [slabs(w, n_conv, conv_idx) for w in weights]
        ff_slabs = [slabs(w, tile_steps, tile_idx) for w in ff]
        inputs = [x] + small + list(weights) + list(ff)
        in_specs = [tok] + small_specs + w_slabs + ff_slabs
        out_shape = (jax.ShapeDtypeStruct((n, d), F32),
                     jax.ShapeDtypeStruct((HEADS, HEAD_DIM, HEAD_DIM), F32),
                     jax.ShapeDtypeStruct((CONV_W - 1, d), F32),
                     *[jax.ShapeDtypeStruct(w.shape, BF16) for w in (*weights, *ff)])
        out_specs = (tok,
                     pl.BlockSpec((HEADS, HEAD_DIM, HEAD_DIM), lambda i: (0, 0, 0)),
                     pl.BlockSpec((CONV_W - 1, d), lambda i: (0, 0)),
                     *[slabs(w, tile_steps, tile_idx) for w in weights], *ff_slabs)
        conv_rows = rows + HIST_ROWS
    else:
        ret_spec = pl.BlockSpec((n_seqs, HEADS, HEAD_DIM, HEAD_DIM), lambda i: (i, 0, 0, 0))
        conv_spec = pl.BlockSpec((n_seqs, CONV_W - 1, d), lambda i: (i, 0, 0))
        inputs = [x, state_ret, state_conv] + small + list(weights)
        in_specs = [tok, ret_spec, conv_spec] + small_specs + [_resident(w.shape) for w in weights]
        out_shape = (jax.ShapeDtypeStruct((n, d), F32),
                     jax.ShapeDtypeStruct(state_ret.shape, F32),
                     jax.ShapeDtypeStruct(state_conv.shape, F32))
        out_specs = (tok, ret_spec, conv_spec)
        conv_rows = n_seqs * (CHUNK + HIST_ROWS)
    scratch = [
        pltpu.VMEM((HEADS, tile, tile), F32),
        pltpu.VMEM((HEADS, tile, HEAD_DIM), F32),
        pltpu.VMEM((HEADS, tile, HEAD_DIM), F32),
        pltpu.VMEM((subtiles, tile, HEAD_DIM), F32),
        pltpu.VMEM((subtiles, tile, HEAD_DIM), F32),
        pltpu.VMEM((conv_rows, d), F32),
    ]
    if carry:
        scratch += [pltpu.VMEM((tile, HEAD_DIM), F32)] * 2
        scratch += [pltpu.VMEM(w.shape, BF16) for w in weights]
    return pl.pallas_call(
        functools.partial(_mixer_kernel, carry, tile, subtiles),
        out_shape=out_shape,
        grid=grid,
        in_specs=in_specs,
        out_specs=out_specs,
        scratch_shapes=scratch,
        compiler_params=pltpu.CompilerParams(
            dimension_semantics=("arbitrary",), vmem_limit_bytes=VMEM_LIMIT_BYTES),
        name="mixer_prompt" if carry else "mixer_sample",
    )(*inputs)


def _mlp_kernel(tile, subtiles, h_ref, n2_ref, w1_ref, w2_ref, nf_ref, y_ref):
    for sub in range(subtiles):
        rows = slice(sub * tile, (sub + 1) * tile)
        h = h_ref[rows, :]
        hn = _rms(h, n2_ref[...]).astype(BF16)
        a = jnp.maximum(_dot(hn, w1_ref[...]), 0.0)
        a2 = (a * a).astype(BF16)
        half = tile // 2
        for r in range(2):
            rs = slice(r * half, (r + 1) * half)
            o = h[rs] + _dot(a2[rs], w2_ref[...])
            y_ref[sub * tile + r * half:sub * tile + (r + 1) * half, :] = _rms(o, nf_ref[...])


def _mlp(h, n2, w1, w2, nf, name):
    n, d = h.shape
    tile, subtiles = MLP_TILE, MLP_SUBTILES
    tok = pl.BlockSpec((tile * subtiles, d), lambda i: (i, 0))
    weights = [n2, w1, w2, nf]
    return pl.pallas_call(
        functools.partial(_mlp_kernel, tile, subtiles),
        out_shape=jax.ShapeDtypeStruct((n, d), F32),
        grid=(n // (tile * subtiles),),
        in_specs=[tok] + [_resident(w.shape) for w in weights],
        out_specs=tok,
        compiler_params=pltpu.CompilerParams(
            dimension_semantics=("arbitrary",), vmem_limit_bytes=VMEM_LIMIT_BYTES),
        name=name,
    )(h, *weights)


def kernel(x_prompt, x_sample, state_ret, state_conv, norm1, w_in, ret_gn_w, conv_w, w_ret_out, w_conv_out, w_o, norm2, w_ff1, w_ff2, norm_f):
    depth = w_in.shape[0]
    assert depth == 1, "single-layer kernel"
    bp, lp, d = x_prompt.shape
    bs, ls, _ = x_sample.shape
    assert bp == 1 and ls == CHUNK
    assert lp % (MIXER_TILE * PROMPT_SUBTILES) == 0 and (bs * ls) % (MIXER_TILE * SAMPLE_SUBTILES) == 0
    assert lp % (MLP_TILE * MLP_SUBTILES) == 0 and (bs * ls) % (MLP_TILE * MLP_SUBTILES) == 0

    row = lambda v: v.reshape(1, -1)
    small = (row(norm1[0]), row(ret_gn_w[0]), conv_w[0])

    hp, ret_p, conv_p, *bf = _mixer(
        x_prompt.reshape(lp, d), None, None, *small,
        (w_in[0], w_ret_out[0], w_conv_out[0], w_o[0]), ff=(w_ff1[0], w_ff2[0]), carry=True)
    mixer_bf, (ff1_bf, ff2_bf) = bf[:4], bf[4:]
    mlp_w = (row(norm2[0]), ff1_bf, ff2_bf, row(norm_f))
    hs, ret_s, conv_s = _mixer(x_sample.reshape(bs * ls, d), state_ret[0], state_conv[0], *small, mixer_bf, carry=False)

    yp = _mlp(hp, *mlp_w, name="mlp_prompt").reshape(bp, lp, d)
    ys = _mlp(hs, *mlp_w, name="mlp_sample").reshape(bs, ls, d)
    return (yp, ys, ret_p[None, None], conv_p[None, None], ret_s[None], conv_s[None])
```

```python
import functools
import math

import jax
import jax.numpy as jnp
from jax import lax
from jax.experimental import pallas as pl
from jax.experimental.pallas import tpu as pltpu

CHUNK = 64
CHUNK_SHIFT = CHUNK.bit_length() - 1
assert 1 << CHUNK_SHIFT == CHUNK
HEADS = 8
HEAD_DIM = 128
CONV_W = 3
ROPE_BASE = 10000.0
EPS = 1e-6
PAST_LEN = 4096

F32_SUBLANES = 8
BF16_SUBLANES = 16
MXU_WIDTH = 256
VMEM_BYTES = 64 * 1024 * 1024

HIST_ROWS = F32_SUBLANES

MIXER_TILE = 256
PROMPT_SUBTILES = 2
SAMPLE_SUBTILES = 1
MLP_TILE = 512
MLP_SUBTILES = 2
CONVERT_SLAB_ROWS = 64
CONVERT_BUFFERS = 3
VMEM_LIMIT_BYTES = VMEM_BYTES - 1024 * 1024

F32 = jnp.float32
BF16 = jnp.bfloat16


def _log_gamma(h):
    return math.log(1.0 - 2.0 ** (-5.0 - h))


def _rms(x, w):
    ms = jnp.mean(x * x, axis=-1, keepdims=True)
    return (x * lax.rsqrt(ms + EPS)) * w


def _dot(a, b):
    return jnp.dot(a, b, preferred_element_type=F32)


def _dot_nt(a, b):
    return lax.dot_general(a, b, (((1,), (1,)), ((), ())), preferred_element_type=F32)


def _dot_tn(a, b):
    return lax.dot_general(a, b, (((0,), (0,)), ((), ())), preferred_element_type=F32)


def _rope_cos_sin(pos):
    lane = lax.broadcasted_iota(jnp.int32, pos.shape, 1)
    half = HEAD_DIM // 2
    inv = jnp.exp((-math.log(ROPE_BASE)) * (lane & (half - 1)).astype(F32) / half)
    ang = pos * inv
    return jnp.cos(ang), jnp.where(lane < half, -1.0, 1.0) * jnp.sin(ang)


def _mixer_kernel(carry, tile, subtiles, *refs):
    n_chunks = tile // CHUNK
    if carry:
        (x_ref, n1_ref, gnw_ref, cw_ref, winf_ref, wrof_ref, wcof_ref, wof_ref, ff1_ref, ff2_ref,
         h_ref, sret_ref, sconv_ref, winb_ref, wrob_ref, wcob_ref, wob_ref, ff1_bf_ref, ff2_bf_ref,
         dmat, rowdec, coldec, cos_s, sin_s, ubuf,
         cosb_s, sinb_s, win_ref, wro_ref, wco_ref, wo_ref,
         stage_in, stage_ro, stage_co, stage_o, sems) = refs
        sin_ret_ref = sin_conv_ref = None
    else:
        (x_ref, sin_ret_ref, sin_conv_ref, n1_ref, gnw_ref, cw_ref, win_ref, wro_ref, wco_ref, wo_ref,
         h_ref, sret_ref, sconv_ref,
         dmat, rowdec, coldec, cos_s, sin_s, ubuf) = refs
    d_model = x_ref.shape[-1]
    step = pl.program_id(0)
    state_len = tile if carry else CHUNK

    @pl.when(step == 0)
    def _init():
        ii = lax.broadcasted_iota(jnp.int32, (tile, tile), 0)
        jj = lax.broadcasted_iota(jnp.int32, (tile, tile), 1)
        dist = jnp.abs(ii - jj).astype(F32)
        ci, cj = ii >> CHUNK_SHIFT, jj >> CHUNK_SHIFT
        keep = (cj <= ci) if carry else (cj == ci)
        row = lax.broadcasted_iota(jnp.int32, (tile, HEAD_DIM), 0)
        loc = (row if carry else row & (CHUNK - 1)).astype(F32)
        for h in range(HEADS):
            lg = _log_gamma(h)
            dmat[h] = jnp.where(keep, jnp.exp(lg * dist), 0.0)
            rowdec[h] = jnp.exp(lg * (loc + 1.0))
            coldec[h] = jnp.exp(lg * (state_len - 1.0 - loc))
        if carry:
            cosb_s[...], sinb_s[...] = _rope_cos_sin(row.astype(F32))
            sret_ref[...] = jnp.zeros_like(sret_ref)
            ubuf[0:HIST_ROWS, :] = jnp.zeros((HIST_ROWS, d_model), F32)
        else:
            cos, sin = _rope_cos_sin((PAST_LEN + (row & (CHUNK - 1))).astype(F32))
            for sub in range(subtiles):
                cos_s[sub], sin_s[sub] = cos, sin

    def tiles(t):
        w0, w1, w2 = cw_ref[0:1, :], cw_ref[1:2, :], cw_ref[2:3, :]

        for sub in range(subtiles):
            rows = slice(sub * tile, (sub + 1) * tile)
            x = x_ref[rows, :]
            xn = _rms(x, n1_ref[...]).astype(BF16)

            def proj(g, cs=None):
                lo, hi = (0, d_model) if cs is None else (cs.start, cs.stop)
                return _dot(xn, win_ref[:, g * d_model + lo:g * d_model + hi])

            slabs = [slice(c, c + MXU_WIDTH) for c in range(0, d_model, MXU_WIDTH)]

            if carry:
                first = jnp.zeros((F32_SUBLANES, HEAD_DIM), jnp.int32) + (t * subtiles + sub) * tile
                ct, st = _rope_cos_sin(first.astype(F32))
                ct, st = ct[0:1, :], st[0:1, :]
                cb, sb = cosb_s[...], sinb_s[...]
                cos_s[sub] = cb * ct - sb * st
                sin_s[sub] = sb * ct + cb * st

            def rope(z):
                return z * cos_s[sub] + pltpu.roll(z, HEAD_DIM // 2, 1) * sin_s[sub]

            def head_cols(g):
                for cs in slabs:
                    z = proj(g, cs)
                    for c in range(0, MXU_WIDTH, HEAD_DIM):
                        yield z[:, c:c + HEAD_DIM]

            qb = [rope(z).astype(BF16) for z in head_cols(0)]
            kb, kdb = [], []
            for h, z in enumerate(head_cols(1)):
                kr = rope(z) * (HEAD_DIM ** -0.5)
                kb.append(kr.astype(BF16))
                kdb.append((kr * coldec[h]).astype(BF16))
            vb = jnp.concatenate([proj(2, cs).astype(BF16) for cs in slabs], axis=1)

            def scores(h):
                return (_dot_nt(qb[h], kb[h]) * dmat[h]).astype(BF16)

            p_next = scores(0)
            gated = []
            for h in range(HEADS):
                hs = slice(h * HEAD_DIM, (h + 1) * HEAD_DIM)
                decay = math.exp(_log_gamma(h) * state_len)
                p = p_next
                if h + 1 < HEADS:
                    p_next = scores(h + 1)
                if h % 2 == 0:
                    sg = jax.nn.silu(proj(3, slabs[h // 2]))
                lanes = slice((h % 2) * HEAD_DIM, (h % 2 + 1) * HEAD_DIM)
                qh, kdh, vh = qb[h], kdb[h], vb[:, hs]
                o = _dot(p, vh)
                if carry:
                    s0 = sret_ref[h]
                    o = o + _dot(qh, s0.astype(BF16)) * rowdec[h]
                    sret_ref[h] = s0 * decay + _dot_tn(kdh, vh)
                else:
                    inter = []
                    for c in range(n_chunks):
                        rs = slice(c * CHUNK, (c + 1) * CHUNK)
                        seq = sub * n_chunks + c
                        s0 = sin_ret_ref[seq, h]
                        inter.append(_dot(qh[rs], s0.astype(BF16)))
                        sret_ref[seq, h] = s0 * decay + _dot_tn(kdh[rs], vh[rs])
                    o = o + jnp.concatenate(inter, axis=0) * rowdec[h]
                mu = jnp.mean(o, axis=-1, keepdims=True)
                d = o - mu
                var = jnp.mean(d * d, axis=-1, keepdims=True)
                on = (d * lax.rsqrt(var + EPS)) * gnw_ref[:, hs]
                gated.append((on * sg[:, lanes]).astype(BF16))
            gated = jnp.concatenate(gated, axis=1)

            u = jnp.concatenate([proj(5, cs) * proj(6, cs) for cs in slabs], axis=1)

            def taps(base, n, cs):
                return ((w0[:, cs] * ubuf[base - 2:base - 2 + n, cs]
                         + w1[:, cs] * ubuf[base - 1:base - 1 + n, cs]) + w2[:, cs] * ubuf[base:base + n, cs])

            if carry:
                base = HIST_ROWS + sub * tile
                ubuf[base:base + tile, :] = u
                conv = lambda cs: taps(base, tile, cs)
            else:
                bases = []
                for c in range(n_chunks):
                    seq = sub * n_chunks + c
                    cbase = seq * (CHUNK + HIST_ROWS) + HIST_ROWS
                    ubuf[cbase - (CONV_W - 1):cbase, :] = sin_conv_ref[seq]
                    ubuf[cbase:cbase + CHUNK, :] = u[c * CHUNK:(c + 1) * CHUNK]
                    sconv_ref[seq] = ubuf[cbase + CHUNK - (CONV_W - 1):cbase + CHUNK, :]
                    bases.append(cbase)
                conv = lambda cs: jnp.concatenate([taps(b, CHUNK, cs) for b in bases], axis=0)
            conv_in = jnp.concatenate([(proj(4, cs) * conv(cs)).astype(BF16) for cs in slabs], axis=1)

            mix = jnp.concatenate(
                [(jax.nn.sigmoid(proj(7, cs)) * _dot(gated, wro_ref[:, cs])
                  + jax.nn.sigmoid(proj(8, cs)) * _dot(conv_in, wco_ref[:, cs])).astype(BF16)
                 for cs in slabs], axis=1)
            h_ref[rows, :] = x_ref[rows, :] + _dot(mix, wo_ref[...])

        if carry:
            end = subtiles * tile + HIST_ROWS
            sconv_ref[...] = ubuf[end - (CONV_W - 1):end, :]
            ubuf[0:HIST_ROWS, :] = ubuf[end - HIST_ROWS:end, :]

    if carry:
        n_conv = 1

        @pl.when(step < n_conv)
        def _convert():
            pairs = ((winf_ref, stage_in, win_ref), (wrof_ref, stage_ro, wro_ref),
                     (wcof_ref, stage_co, wco_ref), (wof_ref, stage_o, wo_ref))
            n_slots, slab = stage_in.shape[0], stage_in.shape[1]
            n_slabs = win_ref.shape[0] // slab

            def slab_copy(w, s):
                f_ref, stage, _ = pairs[w]
                slot = s % n_slots
                return pltpu.make_async_copy(f_ref.at[pl.ds(s * slab, slab), :], stage.at[slot], sems.at[w, slot])

            for s in range(min(n_slots, n_slabs)):
                for w in range(len(pairs)):
                    slab_copy(w, s).start()
            for s in range(n_slabs):
                for w, (_, stage, w_ref) in enumerate(pairs):
                    slab_copy(w, s).wait()
                    w_ref[s * slab:(s + 1) * slab, :] = stage[s % n_slots].astype(BF16)
                    if s + n_slots < n_slabs:
                        slab_copy(w, s + n_slots).start()

        @pl.when(step >= n_conv)
        def _tiles():
            t = step - n_conv
            ff1_bf_ref[...] = ff1_ref[...].astype(BF16)
            ff2_bf_ref[...] = ff2_ref[...].astype(BF16)
            for b_ref, w_ref in ((winb_ref, win_ref), (wrob_ref, wro_ref),
                                 (wcob_ref, wco_ref), (wob_ref, wo_ref)):
                slab = b_ref.shape[0]
                b_ref[...] = w_ref[pl.ds(pl.multiple_of(t * slab, slab), slab), :]
            tiles(t)
    else:
        tiles(step)


def _resident(shape):
    nd = len(shape)
    return pl.BlockSpec(shape, lambda i: (0,) * nd, pipeline_mode=pl.Buffered(1))


def _mixer(x, state_ret, state_conv, n1, gnw, cw, weights, ff=None, *, carry):
    n, d = x.shape
    tile = MIXER_TILE
    subtiles = PROMPT_SUBTILES if carry else SAMPLE_SUBTILES
    rows = tile * subtiles
    n_seqs = rows // CHUNK
    tile_steps = n // rows
    n_conv = 1 if carry else 0
    grid = (n_conv + tile_steps,)
    tile_idx = lambda i: jnp.maximum(i - n_conv, 0)
    tok = pl.BlockSpec((rows, d), lambda i: (tile_idx(i), 0))
    small = [n1, gnw, cw]
    small_specs = [_resident(w.shape) for w in small]

    def slabs(w, steps, index):
        slab = w.shape[0] // steps
        assert slab * steps == w.shape[0] and slab % BF16_SUBLANES == 0
        return pl.BlockSpec((slab, w.shape[1]), lambda i: (index(i), 0))

    if carry:
        assert all(w.shape[0] % CONVERT_SLAB_ROWS == 0 for w in weights)
        ff_slabs = [slabs(w, tile_steps, tile_idx) for w in ff]
        inputs = [x] + small + list(weights) + list(ff)
        in_specs = [tok] + small_specs + [pl.BlockSpec(memory_space=pl.ANY)] * len(weights) + ff_slabs
        out_shape = (jax.ShapeDtypeStruct((n, d), F32),
                     jax.ShapeDtypeStruct((HEADS, HEAD_DIM, HEAD_DIM), F32),
                     jax.ShapeDtypeStruct((CONV_W - 1, d), F32),
                     *[jax.ShapeDtypeStruct(w.shape, BF16) for w in (*weights, *ff)])
        out_specs = (tok,
                     pl.BlockSpec((HEADS, HEAD_DIM, HEAD_DIM), lambda i: (0, 0, 0)),
                     pl.BlockSpec((CONV_W - 1, d), lambda i: (0, 0)),
                     *[slabs(w, tile_steps, tile_idx) for w in weights], *ff_slabs)
        conv_rows = rows + HIST_ROWS
    else:
        ret_spec = pl.BlockSpec((n_seqs, HEADS, HEAD_DIM, HEAD_DIM), lambda i: (i, 0, 0, 0))
        conv_spec = pl.BlockSpec((n_seqs, CONV_W - 1, d), lambda i: (i, 0, 0))
        inputs = [x, state_ret, state_conv] + small + list(weights)
        in_specs = [tok, ret_spec, conv_spec] + small_specs + [_resident(w.shape) for w in weights]
        out_shape = (jax.ShapeDtypeStruct((n, d), F32),
                     jax.ShapeDtypeStruct(state_ret.shape, F32),
                     jax.ShapeDtypeStruct(state_conv.shape, F32))
        out_specs = (tok, ret_spec, conv_spec)
        conv_rows = n_seqs * (CHUNK + HIST_ROWS)
    scratch = [
        pltpu.VMEM((HEADS, tile, tile), F32),
        pltpu.VMEM((HEADS, tile, HEAD_DIM), F32),
        pltpu.VMEM((HEADS, tile, HEAD_DIM), F32),
        pltpu.VMEM((subtiles, tile, HEAD_DIM), F32),
        pltpu.VMEM((subtiles, tile, HEAD_DIM), F32),
        pltpu.VMEM((conv_rows, d), F32),
    ]
    if carry:
        scratch += [pltpu.VMEM((tile, HEAD_DIM), F32)] * 2
        scratch += [pltpu.VMEM(w.shape, BF16) for w in weights]
        scratch += [pltpu.VMEM((CONVERT_BUFFERS, CONVERT_SLAB_ROWS, w.shape[1]), F32)
                    for w in weights]
        scratch += [pltpu.SemaphoreType.DMA((len(weights), CONVERT_BUFFERS))]
    return pl.pallas_call(
        functools.partial(_mixer_kernel, carry, tile, subtiles),
        out_shape=out_shape,
        grid=grid,
        in_specs=in_specs,
        out_specs=out_specs,
        scratch_shapes=scratch,
        compiler_params=pltpu.CompilerParams(
            dimension_semantics=("arbitrary",), vmem_limit_bytes=VMEM_LIMIT_BYTES),
        name="mixer_prompt" if carry else "mixer_sample",
    )(*inputs)


def _mlp_kernel(tile, subtiles, h_ref, n2_ref, w1_ref, w2_ref, nf_ref, y_ref):
    for sub in range(subtiles):
        rows = slice(sub * tile, (sub + 1) * tile)
        h = h_ref[rows, :]
        hn = _rms(h, n2_ref[...]).astype(BF16)
        a = jnp.maximum(_dot(hn, w1_ref[...]), 0.0)
        a2 = (a * a).astype(BF16)
        half = tile // 2
        for r in range(2):
            rs = slice(r * half, (r + 1) * half)
            o = h[rs] + _dot(a2[rs], w2_ref[...])
            y_ref[sub * tile + r * half:sub * tile + (r + 1) * half, :] = _rms(o, nf_ref[...])


def _mlp(h, n2, w1, w2, nf, name):
    n, d = h.shape
    tile, subtiles = MLP_TILE, MLP_SUBTILES
    tok = pl.BlockSpec((tile * subtiles, d), lambda i: (i, 0))
    weights = [n2, w1, w2, nf]
    return pl.pallas_call(
        functools.partial(_mlp_kernel, tile, subtiles),
        out_shape=jax.ShapeDtypeStruct((n, d), F32),
        grid=(n // (tile * subtiles),),
        in_specs=[tok] + [_resident(w.shape) for w in weights],
        out_specs=tok,
        compiler_params=pltpu.CompilerParams(
            dimension_semantics=("arbitrary",), vmem_limit_bytes=VMEM_LIMIT_BYTES),
        name=name,
    )(h, *weights)


def kernel(x_prompt, x_sample, state_ret, state_conv, norm1, w_in, ret_gn_w, conv_w, w_ret_out, w_conv_out, w_o, norm2, w_ff1, w_ff2, norm_f):
    depth = w_in.shape[0]
    assert depth == 1, "single-layer kernel"
    bp, lp, d = x_prompt.shape
    bs, ls, _ = x_sample.shape
    assert bp == 1 and ls == CHUNK
    assert lp % (MIXER_TILE * PROMPT_SUBTILES) == 0 and (bs * ls) % (MIXER_TILE * SAMPLE_SUBTILES) == 0
    assert lp % (MLP_TILE * MLP_SUBTILES) == 0 and (bs * ls) % (MLP_TILE * MLP_SUBTILES) == 0

    row = lambda v: v.reshape(1, -1)
    small = (row(norm1[0]), row(ret_gn_w[0]), conv_w[0])

    hp, ret_p, conv_p, *bf = _mixer(
        x_prompt.reshape(lp, d), None, None, *small,
        (w_in[0], w_ret_out[0], w_conv_out[0], w_o[0]), ff=(w_ff1[0], w_ff2[0]), carry=True)
    mixer_bf, (ff1_bf, ff2_bf) = bf[:4], bf[4:]
    mlp_w = (row(norm2[0]), ff1_bf, ff2_bf, row(norm_f))
    hs, ret_s, conv_s = _mixer(x_sample.reshape(bs * ls, d), state_ret[0], state_conv[0], *small, mixer_bf, carry=False)

    yp = _mlp(hp, *mlp_w, name="mlp_prompt").reshape(bp, lp, d)
    ys = _mlp(hs, *mlp_w, name="mlp_sample").reshape(bs, ls, d)
    return (yp, ys, ret_p[None, None], conv_p[None, None], ret_s[None], conv_s[None])
```

```python
import functools
import math

import jax
import jax.numpy as jnp
from jax import lax
from jax.experimental import pallas as pl
from jax.experimental.pallas import tpu as pltpu

CHUNK = 64
CHUNK_SHIFT = CHUNK.bit_length() - 1
assert 1 << CHUNK_SHIFT == CHUNK
HEADS = 8
HEAD_DIM = 128
CONV_W = 3
ROPE_BASE = 10000.0
EPS = 1e-6
PAST_LEN = 4096

F32_SUBLANES = 8
BF16_SUBLANES = 16
MXU_WIDTH = 256
VMEM_BYTES = 64 * 1024 * 1024

HIST_ROWS = F32_SUBLANES

MIXER_TILE = 256
PROMPT_SUBTILES = 2
SAMPLE_SUBTILES = 1
MLP_TILE = 512
MLP_SUBTILES = 2
CONVERT_STEPS = 8
VMEM_LIMIT_BYTES = VMEM_BYTES - 1024 * 1024

F32 = jnp.float32
BF16 = jnp.bfloat16


def _log_gamma(h):
    return math.log(1.0 - 2.0 ** (-5.0 - h))


def _rms(x, w):
    ms = jnp.mean(x * x, axis=-1, keepdims=True)
    return (x * lax.rsqrt(ms + EPS)) * w


def _dot(a, b):
    return jnp.dot(a, b, preferred_element_type=F32)


def _dot_nt(a, b):
    return lax.dot_general(a, b, (((1,), (1,)), ((), ())), preferred_element_type=F32)


def _dot_tn(a, b):
    return lax.dot_general(a, b, (((0,), (0,)), ((), ())), preferred_element_type=F32)


def _rope_cos_sin(pos):
    lane = lax.broadcasted_iota(jnp.int32, pos.shape, 1)
    half = HEAD_DIM // 2
    inv = jnp.exp((-math.log(ROPE_BASE)) * (lane & (half - 1)).astype(F32) / half)
    ang = pos * inv
    return jnp.cos(ang), jnp.where(lane < half, -1.0, 1.0) * jnp.sin(ang)


def _mixer_kernel(carry, tile, subtiles, *refs):
    n_chunks = tile // CHUNK
    if carry:
        (x_ref, n1_ref, gnw_ref, cw_ref, winf_ref, wrof_ref, wcof_ref, wof_ref, ff1_ref, ff2_ref,
         h_ref, sret_ref, sconv_ref, winb_ref, wrob_ref, wcob_ref, wob_ref, ff1_bf_ref, ff2_bf_ref,
         dmat, rowdec, coldec, cos_s, sin_s, ubuf,
         cosb_s, sinb_s, win_ref, wro_ref, wco_ref, wo_ref) = refs
        sin_ret_ref = sin_conv_ref = None
    else:
        (x_ref, sin_ret_ref, sin_conv_ref, n1_ref, gnw_ref, cw_ref, win_ref, wro_ref, wco_ref, wo_ref,
         h_ref, sret_ref, sconv_ref,
         dmat, rowdec, coldec, cos_s, sin_s, ubuf) = refs
    d_model = x_ref.shape[-1]
    step = pl.program_id(0)
    state_len = tile if carry else CHUNK

    @pl.when(step == 0)
    def _init():
        ii = lax.broadcasted_iota(jnp.int32, (tile, tile), 0)
        jj = lax.broadcasted_iota(jnp.int32, (tile, tile), 1)
        dist = jnp.abs(ii - jj).astype(F32)
        ci, cj = ii >> CHUNK_SHIFT, jj >> CHUNK_SHIFT
        keep = (cj <= ci) if carry else (cj == ci)
        row = lax.broadcasted_iota(jnp.int32, (tile, HEAD_DIM), 0)
        loc = (row if carry else row & (CHUNK - 1)).astype(F32)
        for h in range(HEADS):
            lg = _log_gamma(h)
            dmat[h] = jnp.where(keep, jnp.exp(lg * dist), 0.0)
            rowdec[h] = jnp.exp(lg * (loc + 1.0))
            coldec[h] = jnp.exp(lg * (state_len - 1.0 - loc))
        if carry:
            cosb_s[...], sinb_s[...] = _rope_cos_sin(row.astype(F32))
            sret_ref[...] = jnp.zeros_like(sret_ref)
            ubuf[0:HIST_ROWS, :] = jnp.zeros((HIST_ROWS, d_model), F32)
        else:
            cos, sin = _rope_cos_sin((PAST_LEN + (row & (CHUNK - 1))).astype(F32))
            for sub in range(subtiles):
                cos_s[sub], sin_s[sub] = cos, sin

    def tiles(t):
        w0, w1, w2 = cw_ref[0:1, :], cw_ref[1:2, :], cw_ref[2:3, :]

        for sub in range(subtiles):
            rows = slice(sub * tile, (sub + 1) * tile)
            x = x_ref[rows, :]
            xn = _rms(x, n1_ref[...]).astype(BF16)

            def proj(g, cs=None):
                lo, hi = (0, d_model) if cs is None else (cs.start, cs.stop)
                return _dot(xn, win_ref[:, g * d_model + lo:g * d_model + hi])

            slabs = [slice(c, c + MXU_WIDTH) for c in range(0, d_model, MXU_WIDTH)]

            if carry:
                first = jnp.zeros((F32_SUBLANES, HEAD_DIM), jnp.int32) + (t * subtiles + sub) * tile
                ct, st = _rope_cos_sin(first.astype(F32))
                ct, st = ct[0:1, :], st[0:1, :]
                cb, sb = cosb_s[...], sinb_s[...]
                cos_s[sub] = cb * ct - sb * st
                sin_s[sub] = sb * ct + cb * st

            def rope(z):
                return z * cos_s[sub] + pltpu.roll(z, HEAD_DIM // 2, 1) * sin_s[sub]

            def head_cols(g):
                for cs in slabs:
                    z = proj(g, cs)
                    for c in range(0, MXU_WIDTH, HEAD_DIM):
                        yield z[:, c:c + HEAD_DIM]

            qb = [rope(z).astype(BF16) for z in head_cols(0)]
            kb, kdb = [], []
            for h, z in enumerate(head_cols(1)):
                kr = rope(z) * (HEAD_DIM ** -0.5)
                kb.append(kr.astype(BF16))
                kdb.append((kr * coldec[h]).astype(BF16))
            vb = jnp.concatenate([proj(2, cs).astype(BF16) for cs in slabs], axis=1)

            def scores(h):
                return (_dot_nt(qb[h], kb[h]) * dmat[h]).astype(BF16)

            p_next = scores(0)
            gated = []
            for h in range(HEADS):
                hs = slice(h * HEAD_DIM, (h + 1) * HEAD_DIM)
                decay = math.exp(_log_gamma(h) * state_len)
                p = p_next
                if h + 1 < HEADS:
                    p_next = scores(h + 1)
                if h % 2 == 0:
                    sg = jax.nn.silu(proj(3, slabs[h // 2]))
                lanes = slice((h % 2) * HEAD_DIM, (h % 2 + 1) * HEAD_DIM)
                qh, kdh, vh = qb[h], kdb[h], vb[:, hs]
                o = _dot(p, vh)
                if carry:
                    s0 = sret_ref[h]
                    o = o + _dot(qh, s0.astype(BF16)) * rowdec[h]
                    sret_ref[h] = s0 * decay + _dot_tn(kdh, vh)
                else:
                    inter = []
                    for c in range(n_chunks):
                        rs = slice(c * CHUNK, (c + 1) * CHUNK)
                        seq = sub * n_chunks + c
                        s0 = sin_ret_ref[seq, h]
                        inter.append(_dot(qh[rs], s0.astype(BF16)))
                        sret_ref[seq, h] = s0 * decay + _dot_tn(kdh[rs], vh[rs])
                    o = o + jnp.concatenate(inter, axis=0) * rowdec[h]
                parts = []
                for r in range(2):
                    rs = slice(r * tile // 2, (r + 1) * tile // 2)
                    mu = jnp.mean(o[rs], axis=-1, keepdims=True)
                    d = o[rs] - mu
                    var = jnp.mean(d * d, axis=-1, keepdims=True)
                    on = (d * lax.rsqrt(var + EPS)) * gnw_ref[:, hs]
                    parts.append((on * sg[rs, lanes]).astype(BF16))
                gated.append(jnp.concatenate(parts, axis=0))
            gated = jnp.concatenate(gated, axis=1)

            u = jnp.concatenate([proj(5, cs) * proj(6, cs) for cs in slabs], axis=1)

            def taps(base, n, cs):
                return ((w0[:, cs] * ubuf[base - 2:base - 2 + n, cs]
                         + w1[:, cs] * ubuf[base - 1:base - 1 + n, cs]) + w2[:, cs] * ubuf[base:base + n, cs])

            if carry:
                base = HIST_ROWS + sub * tile
                ubuf[base:base + tile, :] = u
                conv = lambda cs: taps(base, tile, cs)
            else:
                bases = []
                for c in range(n_chunks):
                    seq = sub * n_chunks + c
                    cbase = seq * (CHUNK + HIST_ROWS) + HIST_ROWS
                    ubuf[cbase - (CONV_W - 1):cbase, :] = sin_conv_ref[seq]
                    ubuf[cbase:cbase + CHUNK, :] = u[c * CHUNK:(c + 1) * CHUNK]
                    sconv_ref[seq] = ubuf[cbase + CHUNK - (CONV_W - 1):cbase + CHUNK, :]
                    bases.append(cbase)
                conv = lambda cs: jnp.concatenate([taps(b, CHUNK, cs) for b in bases], axis=0)
            conv_in = jnp.concatenate([(proj(4, cs) * conv(cs)).astype(BF16) for cs in slabs], axis=1)

            mix = jnp.concatenate(
                [(jax.nn.sigmoid(proj(7, cs)) * _dot(gated, wro_ref[:, cs])
                  + jax.nn.sigmoid(proj(8, cs)) * _dot(conv_in, wco_ref[:, cs])).astype(BF16)
                 for cs in slabs], axis=1)
            h_ref[rows, :] = x_ref[rows, :] + _dot(mix, wo_ref[...])

        if carry:
            end = subtiles * tile + HIST_ROWS
            sconv_ref[...] = ubuf[end - (CONV_W - 1):end, :]
            ubuf[0:HIST_ROWS, :] = ubuf[end - HIST_ROWS:end, :]

    if carry:
        n_conv = win_ref.shape[0] // winf_ref.shape[0]

        @pl.when(step < n_conv)
        def _convert():
            for f_ref, w_ref in ((winf_ref, win_ref), (wrof_ref, wro_ref),
                                 (wcof_ref, wco_ref), (wof_ref, wo_ref)):
                slab = f_ref.shape[0]
                w_ref[pl.ds(pl.multiple_of(step * slab, slab), slab), :] = f_ref[...].astype(BF16)

        @pl.when(step >= n_conv)
        def _tiles():
            t = step - n_conv
            ff1_bf_ref[...] = ff1_ref[...].astype(BF16)
            ff2_bf_ref[...] = ff2_ref[...].astype(BF16)
            for b_ref, w_ref in ((winb_ref, win_ref), (wrob_ref, wro_ref),
                                 (wcob_ref, wco_ref), (wob_ref, wo_ref)):
                slab = b_ref.shape[0]
                b_ref[...] = w_ref[pl.ds(pl.multiple_of(t * slab, slab), slab), :]
            tiles(t)
    else:
        tiles(step)


def _resident(shape):
    nd = len(shape)
    return pl.BlockSpec(shape, lambda i: (0,) * nd, pipeline_mode=pl.Buffered(1))


def _mixer(x, state_ret, state_conv, n1, gnw, cw, weights, ff=None, *, carry):
    n, d = x.shape
    tile = MIXER_TILE
    subtiles = PROMPT_SUBTILES if carry else SAMPLE_SUBTILES
    rows = tile * subtiles
    n_seqs = rows // CHUNK
    tile_steps = n // rows
    n_conv = CONVERT_STEPS if carry else 0
    grid = (n_conv + tile_steps,)
    tile_idx = lambda i: jnp.maximum(i - n_conv, 0)
    tok = pl.BlockSpec((rows, d), lambda i: (tile_idx(i), 0))
    small = [n1, gnw, cw]
    small_specs = [_resident(w.shape) for w in small]

    def slabs(w, steps, index):
        slab = w.shape[0] // steps
        assert slab * steps == w.shape[0] and slab % BF16_SUBLANES == 0
        return pl.BlockSpec((slab, w.shape[1]), lambda i: (index(i), 0))

    if carry:
        conv_idx = lambda i: jnp.minimum(i, n_conv - 1)
        w_slabs = [slabs(w, n_conv, conv_idx) for w in weights]
        ff_slabs = [slabs(w, tile_steps, tile_idx) for w in ff]
        inputs = [x] + small + list(weights) + list(ff)
        in_specs = [tok] + small_specs + w_slabs + ff_slabs
        out_shape = (jax.ShapeDtypeStruct((n, d), F32),
                     jax.ShapeDtypeStruct((HEADS, HEAD_DIM, HEAD_DIM), F32),
                     jax.ShapeDtypeStruct((CONV_W - 1, d), F32),
                     *[jax.ShapeDtypeStruct(w.shape, BF16) for w in (*weights, *ff)])
        out_specs = (tok,
                     pl.BlockSpec((HEADS, HEAD_DIM, HEAD_DIM), lambda i: (0, 0, 0)),
                     pl.BlockSpec((CONV_W - 1, d), lambda i: (0, 0)),
                     *[slabs(w, tile_steps, tile_idx) for w in weights], *ff_slabs)
        conv_rows = rows + HIST_ROWS
    else:
        ret_spec = pl.BlockSpec((n_seqs, HEADS, HEAD_DIM, HEAD_DIM), lambda i: (i, 0, 0, 0))
        conv_spec = pl.BlockSpec((n_seqs, CONV_W - 1, d), lambda i: (i, 0, 0))
        inputs = [x, state_ret, state_conv] + small + list(weights)
        in_specs = [tok, ret_spec, conv_spec] + small_specs + [_resident(w.shape) for w in weights]
        out_shape = (jax.ShapeDtypeStruct((n, d), F32),
                     jax.ShapeDtypeStruct(state_ret.shape, F32),
                     jax.ShapeDtypeStruct(state_conv.shape, F32))
        out_specs = (tok, ret_spec, conv_spec)
        conv_rows = n_seqs * (CHUNK + HIST_ROWS)
    scratch = [
        pltpu.VMEM((HEADS, tile, tile), F32),
        pltpu.VMEM((HEADS, tile, HEAD_DIM), F32),
        pltpu.VMEM((HEADS, tile, HEAD_DIM), F32),
        pltpu.VMEM((subtiles, tile, HEAD_DIM), F32),
        pltpu.VMEM((subtiles, tile, HEAD_DIM), F32),
        pltpu.VMEM((conv_rows, d), F32),
    ]
    if carry:
        scratch += [pltpu.VMEM((tile, HEAD_DIM), F32)] * 2
        scratch += [pltpu.VMEM(w.shape, BF16) for w in weights]
    return pl.pallas_call(
        functools.partial(_mixer_kernel, carry, tile, subtiles),
        out_shape=out_shape,
        grid=grid,
        in_specs=in_specs,
        out_specs=out_specs,
        scratch_shapes=scratch,
        compiler_params=pltpu.CompilerParams(
            dimension_semantics=("arbitrary",), vmem_limit_bytes=VMEM_LIMIT_BYTES),
        name="mixer_prompt" if carry else "mixer_sample",
    )(*inputs)


def _mlp_kernel(tile, subtiles, h_ref, n2_ref, w1_ref, w2_ref, nf_ref, y_ref):
    for sub in range(subtiles):
        rows = slice(sub * tile, (sub + 1) * tile)
        h = h_ref[rows, :]
        hn = _rms(h, n2_ref[...]).astype(BF16)
        a = jnp.maximum(_dot(hn, w1_ref[...]), 0.0)
        a2 = (a * a).astype(BF16)
        half = tile // 2
        for r in range(2):
            rs = slice(r * half, (r + 1) * half)
            o = h[rs] + _dot(a2[rs], w2_ref[...])
            y_ref[sub * tile + r * half:sub * tile + (r + 1) * half, :] = _rms(o, nf_ref[...])


def _mlp(h, n2, w1, w2, nf, name):
    n, d = h.shape
    tile, subtiles = MLP_TILE, MLP_SUBTILES
    tok = pl.BlockSpec((tile * subtiles, d), lambda i: (i, 0))
    weights = [n2, w1, w2, nf]
    return pl.pallas_call(
        functools.partial(_mlp_kernel, tile, subtiles),
        out_shape=jax.ShapeDtypeStruct((n, d), F32),
        grid=(n // (tile * subtiles),),
        in_specs=[tok] + [_resident(w.shape) for w in weights],
        out_specs=tok,
        compiler_params=pltpu.CompilerParams(
            dimension_semantics=("arbitrary",), vmem_limit_bytes=VMEM_LIMIT_BYTES),
        name=name,
    )(h, *weights)


def kernel(x_prompt, x_sample, state_ret, state_conv, norm1, w_in, ret_gn_w, conv_w, w_ret_out, w_conv_out, w_o, norm2, w_ff1, w_ff2, norm_f):
    depth = w_in.shape[0]
    assert depth == 1, "single-layer kernel"
    bp, lp, d = x_prompt.shape
    bs, ls, _ = x_sample.shape
    assert bp == 1 and ls == CHUNK
    assert lp % (MIXER_TILE * PROMPT_SUBTILES) == 0 and (bs * ls) % (MIXER_TILE * SAMPLE_SUBTILES) == 0
    assert lp % (MLP_TILE * MLP_SUBTILES) == 0 and (bs * ls) % (MLP_TILE * MLP_SUBTILES) == 0

    row = lambda v: v.reshape(1, -1)
    small = (row(norm1[0]), row(ret_gn_w[0]), conv_w[0])

    hp, ret_p, conv_p, *bf = _mixer(
        x_prompt.reshape(lp, d), None, None, *small,
        (w_in[0], w_ret_out[0], w_conv_out[0], w_o[0]), ff=(w_ff1[0], w_ff2[0]), carry=True)
    mixer_bf, (ff1_bf, ff2_bf) = bf[:4], bf[4:]
    mlp_w = (row(norm2[0]), ff1_bf, ff2_bf, row(norm_f))
    hs, ret_s, conv_s = _mixer(x_sample.reshape(bs * ls, d), state_ret[0], state_conv[0], *small, mixer_bf, carry=False)

    yp = _mlp(hp, *mlp_w, name="mlp_prompt").reshape(bp, lp, d)
    ys = _mlp(hs, *mlp_w, name="mlp_sample").reshape(bs, ls, d)
    return (yp, ys, ret_p[None, None], conv_p[None, None], ret_s[None], conv_s[None])
```
